```python
import math, functools
import jax, jax.numpy as jnp
from jax import lax
import numpy as np

D_MODEL = 1024
BATCH = 2
SEQ = 8192
DEPTH = 2
DEC_BATCH = 128
DEC_SEQ = 8
PAST_LEN = 2048
PAGE_SIZE = 128

DA_HEADS = 8
DA_DK = 64
DA_DV = 2 * DA_DK
DA_WIDTH = DA_HEADS * DA_DV
ROPE_THETA = 10000.0
Q_BLOCK = 128
GLA_HEADS = 4
GLA_KEY = D_MODEL // 2
GLA_VAL = D_MODEL
GLA_DK = GLA_KEY // GLA_HEADS
GLA_DV = GLA_VAL // GLA_HEADS
GLA_RANK = 16
GLA_TAU = 16.0
GLA_CHUNK = 32
NORM_EPS = 1e-5
DEEPNORM_ALPHA = (2.0 * DEPTH) ** 0.25
DEEPNORM_BETA = (8.0 * DEPTH) ** -0.25
IN_SPLITS = (DA_WIDTH, DA_WIDTH, DA_WIDTH, DA_WIDTH,
             GLA_KEY, GLA_KEY, GLA_VAL, GLA_VAL,
             GLA_RANK,
             D_MODEL, D_MODEL)
D_IN = sum(IN_SPLITS)

kernel_name = 'gated_diffattn_gla_hybrid'


def rmsnorm(x, w):
    xf = x.astype(jnp.float32)
    xf = xf * lax.rsqrt(jnp.mean(xf * xf, axis=-1, keepdims=True) + NORM_EPS)
    return (xf * w.astype(jnp.float32)).astype(x.dtype)


def layernorm(x, w, b):
    xf = x.astype(jnp.float32)
    mu = jnp.mean(xf, axis=-1, keepdims=True)
    var = jnp.mean(jnp.square(xf - mu), axis=-1, keepdims=True)
    y = (xf - mu) * lax.rsqrt(var + NORM_EPS) * w.astype(jnp.float32) + b.astype(jnp.float32)
    return y.astype(x.dtype)


def rope(x, pos):
    d = x.shape[-1]
    inv = ROPE_THETA ** (-jnp.arange(0, d, 2, dtype=jnp.float32) / d)
    ang = pos.astype(jnp.float32)[:, None] * inv[None, :]
    shp = (1, pos.shape[0]) + (1,) * (x.ndim - 3) + (d // 2,)
    cos, sin = jnp.cos(ang).reshape(shp), jnp.sin(ang).reshape(shp)
    xf = x.astype(jnp.float32)
    x1, x2 = xf[..., : d // 2], xf[..., d // 2:]
    return jnp.concatenate([x1 * cos - x2 * sin, x2 * cos + x1 * sin], axis=-1).astype(x.dtype)


def diff_probs(s, lam):
    p = jax.nn.softmax(s, axis=-1)
    return p[0] - lam * p[1]


def diff_attn_prompt(q, k, v, lam):
    B, T, H = q.shape[0], q.shape[1], q.shape[2]
    nb = T // Q_BLOCK
    kpos = jnp.arange(T)
    qb = jnp.moveaxis(q.reshape(B, nb, Q_BLOCK, H, 2, DA_DK), 1, 0)

    def one_block(args):
        qi, i = args
        s = jnp.einsum('bqhcd,bkhcd->cbhqk', qi, k).astype(jnp.float32) * (DA_DK ** -0.5)
        qpos = i * Q_BLOCK + jnp.arange(Q_BLOCK)
        mask = kpos[None, :] <= qpos[:, None]
        p = diff_probs(jnp.where(mask, s, -1e30), lam)
        return jnp.einsum('bhqk,bkhd->bqhd', p.astype(v.dtype), v)

    o = lax.map(one_block, (qb, jnp.arange(nb)))
    return jnp.moveaxis(o, 0, 1).reshape(B, T, H, DA_DV)


def diff_attn_sample(q, k, v, lam, k_past, v_past):
    T = q.shape[1]
    P = k_past.shape[1]
    scale = DA_DK ** -0.5
    s_past = jnp.einsum('bqhcd,bkhcd->cbhqk', q, k_past).astype(jnp.float32) * scale
    s_new = jnp.einsum('bqhcd,bkhcd->cbhqk', q, k).astype(jnp.float32) * scale
    causal = jnp.tril(jnp.ones((T, T), dtype=bool))
    s = jnp.concatenate([s_past, jnp.where(causal, s_new, -1e30)], axis=-1)
    p = diff_probs(s, lam).astype(v.dtype)
    return (jnp.einsum('bhqk,bkhd->bqhd', p[..., :P], v_past)
            + jnp.einsum('bhqk,bkhd->bqhd', p[..., P:], v))


def gla_chunked(q, k, v, log_a, s0):
    B, T, H, DK = q.shape
    DV = v.shape[-1]
    C = min(GLA_CHUNK, T)
    pad = (-T) % C
    f32 = jnp.float32
    qf, kf, vf, la = (a.astype(f32) for a in (q, k, v, log_a))
    if pad:
        widths = ((0, 0), (0, pad), (0, 0), (0, 0))
        qf, kf, vf, la = (jnp.pad(a, widths) for a in (qf, kf, vf, la))
    n = (T + pad) // C
    qf = qf.reshape(B, n, C, H, DK)
    kf = kf.reshape(B, n, C, H, DK)
    vf = vf.reshape(B, n, C, H, DV)
    b = jnp.cumsum(la.reshape(B, n, C, H, DK), axis=2)
    b_last = b[:, :, -1]
    q_dec = qf * jnp.exp(b) * (DK ** -0.5)
    k_dec = kf * jnp.exp(-b)
    causal = jnp.tril(jnp.ones((C, C), dtype=bool))
    a = jnp.where(causal, jnp.einsum('bnthk,bnshk->bnhts', q_dec, k_dec), 0.0)
    o_intra = jnp.einsum('bnhts,bnshv->bnthv', a, vf)
    k_end = kf * jnp.exp(b_last[:, :, None] - b)
    ds = jnp.einsum('bnshk,bnshv->nbhkv', k_end, vf)

    def step(s, inp):
        ds_c, bl_c = inp
        return jnp.exp(bl_c)[..., None] * s + ds_c, s

    s_fin, s_start = lax.scan(step, s0.astype(f32), (ds, jnp.moveaxis(b_last, 1, 0)))
    o_inter = jnp.einsum('bnthk,nbhkv->bnthv', q_dec, s_start)
    o = (o_intra + o_inter).reshape(B, n * C, H, DV)[:, :T]
    return o.astype(v.dtype), s_fin.astype(s0.dtype)


def gather_pages(pool, page_table):
    g = pool[page_table]
    return g.reshape((g.shape[0], g.shape[1] * g.shape[2]) + g.shape[3:])


def mixer_layer(x, pos, layer, attend, s0, w_in, lam_q1, lam_k1, lam_q2, lam_k2,
                da_norm_w, gla_w_a2, gla_b_a, gla_norm_w, w_out, ln_w, ln_b):
    B, T, _ = x.shape
    z = jnp.einsum('btd,de->bte', x, w_in[layer])
    points = np.cumsum(IN_SPLITS)[:-1].tolist()
    dq, dk, dv, dg, gq, gk, gv, gg, g_lr, m_a, m_b = jnp.split(z, points, axis=-1)
    q = rope(dq.reshape(B, T, DA_HEADS, 2, DA_DK), pos)
    k = rope(dk.reshape(B, T, DA_HEADS, 2, DA_DK), pos)
    v = dv.reshape(B, T, DA_HEADS, DA_DV)
    lam_init = 0.8 - 0.6 * math.exp(-0.3 * layer)
    f32 = jnp.float32
    lam = (jnp.exp(jnp.sum(lam_q1[layer].astype(f32) * lam_k1[layer].astype(f32)))
           - jnp.exp(jnp.sum(lam_q2[layer].astype(f32) * lam_k2[layer].astype(f32))) + lam_init)
    o_a = rmsnorm(attend(q, k, v, lam), da_norm_w[layer]) * (1.0 - lam_init)
    y_a = o_a.reshape(B, T, DA_WIDTH) * jax.nn.silu(dg)
    gate_logit = jnp.einsum('btr,rk->btk', g_lr, gla_w_a2[layer]) + gla_b_a[layer]
    log_a = jax.nn.log_sigmoid(gate_logit.astype(f32)) / GLA_TAU
    o_b, s_fin = gla_chunked(gq.reshape(B, T, GLA_HEADS, GLA_DK), gk.reshape(B, T, GLA_HEADS, GLA_DK),
                             gv.reshape(B, T, GLA_HEADS, GLA_DV), log_a.reshape(B, T, GLA_HEADS, GLA_DK), s0)
    y_b = rmsnorm(o_b, gla_norm_w[layer]).reshape(B, T, GLA_VAL) * jax.nn.silu(gg)
    merged = jax.nn.sigmoid(m_a) * y_a + jax.nn.sigmoid(m_b) * y_b
    h = jnp.einsum('bte,ed->btd', merged, w_out[layer])
    x_new = layernorm(DEEPNORM_ALPHA * x + h, ln_w[layer], ln_b[layer])
    return x_new, k, v, s_fin


def setup_inputs(seed: int = 0) -> dict:
    key = jax.random.key(seed)
    ks = jax.random.split(key, 20)
    f32 = jnp.float32
    n_pages = PAST_LEN // PAGE_SIZE
    n_used = DEC_BATCH * n_pages
    n_pool = n_used + n_used // 4
    x_prompt = jax.random.normal(ks[0], (BATCH, SEQ, D_MODEL), f32)
    x_sample = jax.random.normal(ks[1], (DEC_BATCH, DEC_SEQ, D_MODEL), f32)
    cache_k = jax.random.normal(ks[2], (DEPTH, n_pool, PAGE_SIZE, DA_HEADS, 2, DA_DK), f32)
    cache_v = jax.random.normal(ks[3], (DEPTH, n_pool, PAGE_SIZE, DA_HEADS, DA_DV), f32)
    state_gla = jax.random.normal(ks[4], (DEPTH, DEC_BATCH, GLA_HEADS, GLA_DK, GLA_DV), f32)
    page_table = jax.random.permutation(ks[5], n_pool)[:n_used].reshape(DEC_BATCH, n_pages).astype(jnp.int32)
    col_scale = np.concatenate([np.full((s,), DEEPNORM_BETA if i in (2, 6) else 1.0, np.float32)
                                for i, s in enumerate(IN_SPLITS)])
    w_in = jax.random.normal(ks[6], (DEPTH, D_MODEL, D_IN), f32) * (D_MODEL ** -0.5) * jnp.asarray(col_scale)
    lam_q1 = 0.1 * jax.random.normal(ks[7], (DEPTH, DA_DK), f32)
    lam_k1 = 0.1 * jax.random.normal(ks[8], (DEPTH, DA_DK), f32)
    lam_q2 = 0.1 * jax.random.normal(ks[9], (DEPTH, DA_DK), f32)
    lam_k2 = 0.1 * jax.random.normal(ks[10], (DEPTH, DA_DK), f32)
    da_norm_w = 1.0 + 0.01 * jax.random.normal(ks[11], (DEPTH, DA_DV), f32)
    gla_w_a2 = jax.random.normal(ks[12], (DEPTH, GLA_RANK, GLA_KEY), f32) * (GLA_RANK ** -0.5)
    gla_b_a = 0.1 * jax.random.normal(ks[13], (DEPTH, GLA_KEY), f32)
    gla_norm_w = 1.0 + 0.01 * jax.random.normal(ks[14], (DEPTH, GLA_DV), f32)
    w_out = jax.random.normal(ks[15], (DEPTH, D_MODEL, D_MODEL), f32) * (D_MODEL ** -0.5) * DEEPNORM_BETA
    ln_w = 1.0 + 0.01 * jax.random.normal(ks[16], (DEPTH, D_MODEL), f32)
    ln_b = 0.01 * jax.random.normal(ks[17], (DEPTH, D_MODEL), f32)
    return {'x_prompt': x_prompt, 'x_sample': x_sample, 'cache_k': cache_k, 'cache_v': cache_v,
            'state_gla': state_gla, 'page_table': page_table, 'w_in': w_in,
            'lam_q1': lam_q1, 'lam_k1': lam_k1, 'lam_q2': lam_q2, 'lam_k2': lam_k2,
            'da_norm_w': da_norm_w, 'gla_w_a2': gla_w_a2, 'gla_b_a': gla_b_a, 'gla_norm_w': gla_norm_w,
            'w_out': w_out, 'ln_w': ln_w, 'ln_b': ln_b}


def reference(x_prompt, x_sample, cache_k, cache_v, state_gla, page_table, w_in,
              lam_q1, lam_k1, lam_q2, lam_k2, da_norm_w, gla_w_a2, gla_b_a, gla_norm_w,
              w_out, ln_w, ln_b):
    pos_p = jnp.arange(x_prompt.shape[1])
    pos_s = PAST_LEN + jnp.arange(x_sample.shape[1])
    xp, xs = x_prompt, x_sample
    kp_rows, vp_rows, sp_rows, ks_rows, vs_rows, ss_rows = [], [], [], [], [], []
    for layer in range(DEPTH):
        s0_p = jnp.zeros((x_prompt.shape[0], GLA_HEADS, GLA_DK, GLA_DV), state_gla.dtype)
        xp, k_p, v_p, s_p = mixer_layer(xp, pos_p, layer, diff_attn_prompt, s0_p, w_in,
                                        lam_q1, lam_k1, lam_q2, lam_k2, da_norm_w, gla_w_a2,
                                        gla_b_a, gla_norm_w, w_out, ln_w, ln_b)
        k_past = gather_pages(cache_k[layer], page_table)
        v_past = gather_pages(cache_v[layer], page_table)
        attend_s = functools.partial(diff_attn_sample, k_past=k_past, v_past=v_past)
        xs, k_s, v_s, s_s = mixer_layer(xs, pos_s, layer, attend_s, state_gla[layer], w_in,
                                        lam_q1, lam_k1, lam_q2, lam_k2, da_norm_w, gla_w_a2,
                                        gla_b_a, gla_norm_w, w_out, ln_w, ln_b)
        kp_rows.append(k_p); vp_rows.append(v_p); sp_rows.append(s_p)
        ks_rows.append(k_s); vs_rows.append(v_s); ss_rows.append(s_s)
    return (xp, xs, jnp.stack(kp_rows), jnp.stack(vp_rows), jnp.stack(sp_rows),
            jnp.stack(ks_rows), jnp.stack(vs_rows), jnp.stack(ss_rows))
```

```python
import functools
import math

import jax
import jax.numpy as jnp
from jax import lax
from jax.experimental import pallas as pl
from jax.experimental.pallas import tpu as pltpu

D_MODEL = 1024
DEPTH = 2
PAST_LEN = 2048
PAGE_SIZE = 128
N_PAGES = PAST_LEN // PAGE_SIZE
DA_HEADS = 8
DA_DK = 64
DA_DV = 128
ROPE_THETA = 10000.0
GLA_HEADS = 4
GLA_KEY = 512
GLA_DK = 128
GLA_DV = 256
GLA_RANK = 16
GLA_TAU = 16.0
GLA_CHUNK = 32
NORM_EPS = 1e-5
DEEPNORM_ALPHA = (2.0 * DEPTH) ** 0.25
MASK_VALUE = -1e30

LANES = 128
SUBLANES = 8
VMEM_LIMIT = 56 * 1024 * 1024

F32 = jnp.float32
BF16 = jnp.bfloat16
NT_DIMS = (((1,), (1,)), ((), ()))
TN_DIMS = (((0,), (0,)), ((), ()))


def _params(*sem):
    return pltpu.CompilerParams(dimension_semantics=sem, vmem_limit_bytes=VMEM_LIMIT)


def _sigmoid(x):
    return 1.0 / (1.0 + jnp.exp(-x))


def _rope_lanes(zj, cos, sina, sinb):
    return zj * cos + pltpu.roll(zj, LANES - 32, 1) * sina + pltpu.roll(zj, 32, 1) * sinb


def _proj_attn_kernel(x_ref, wq_ref, wkt_ref, wv_ref, cos_ref, sina_ref, sinb_ref, cost_ref, sint_ref,
                      q_ref, k_ref, *rest, k_transposed):
    xb = x_ref[0].astype(BF16)
    cos, sina, sinb = cos_ref[...], sina_ref[...], sinb_ref[...]
    zq = jnp.dot(xb, wq_ref[...], preferred_element_type=F32)
    for j in range(D_MODEL // LANES):
        sl = slice(j * LANES, (j + 1) * LANES)
        q_ref[0, :, sl] = (_rope_lanes(zq[:, sl], cos, sina, sinb) * (DA_DK ** -0.5)).astype(BF16)
    if k_transposed:
        kb_ref, vf_ref, vb_ref = rest
        zk = lax.dot_general(wkt_ref[...], xb, NT_DIMS, preferred_element_type=F32)
        tm = zk.shape[1]
        half = DA_DK // 2
        z4 = zk.reshape(2 * DA_HEADS, 2, half, tm)
        x1, x2 = z4[:, 0], z4[:, 1]
        ct, st = cost_ref[...][None], sint_ref[...][None]
        kt = jnp.stack([x1 * ct - x2 * st, x2 * ct + x1 * st], axis=1).reshape(D_MODEL, tm)
        k_ref[0] = kt
        kb_ref[0, 0] = kt.astype(BF16)
    else:
        vf_ref, vb_ref = rest
        zk = lax.dot_general(xb, wkt_ref[...], NT_DIMS, preferred_element_type=F32)
        for j in range(D_MODEL // LANES):
            sl = slice(j * LANES, (j + 1) * LANES)
            k_ref[0, :, sl] = _rope_lanes(zk[:, sl], cos, sina, sinb)
    v = jnp.dot(xb, wv_ref[...], preferred_element_type=F32)
    vf_ref[0] = v
    vb_ref[0] = v.astype(BF16)


def _proj_attn(x, wq, wkt, wv, tables, tables_t, tm, k_transposed):
    b, t, _ = x.shape
    nt = tables[0].shape[0] // tm
    row = lambda bi, i: (bi, i, 0)
    tab = lambda bi, i: (i % nt, 0)
    tab_t = lambda bi, i: (0, i % nt)
    wspec = pl.BlockSpec((D_MODEL, D_MODEL), lambda bi, i: (0, 0))
    tok = pl.BlockSpec((1, tm, D_MODEL), row)
    tok_f32 = jax.ShapeDtypeStruct((b, t, D_MODEL), F32)
    tok_bf16 = jax.ShapeDtypeStruct((b, t, D_MODEL), BF16)
    if k_transposed:
        k_specs = [pl.BlockSpec((1, D_MODEL, tm), lambda bi, i: (bi, 0, i)),
                   pl.BlockSpec((1, 1, D_MODEL, tm), lambda bi, i: (bi, i, 0, 0))]
        k_shapes = [jax.ShapeDtypeStruct((b, D_MODEL, t), F32),
                    jax.ShapeDtypeStruct((b, t // tm, D_MODEL, tm), BF16)]
    else:
        k_specs, k_shapes = [tok], [tok_f32]
    return pl.pallas_call(
        functools.partial(_proj_attn_kernel, k_transposed=k_transposed),
        grid=(b, t // tm),
        in_specs=[tok, wspec, wspec, wspec,
                  pl.BlockSpec((tm, LANES), tab), pl.BlockSpec((tm, LANES), tab), pl.BlockSpec((tm, LANES), tab),
                  pl.BlockSpec((DA_DK // 2, tm), tab_t), pl.BlockSpec((DA_DK // 2, tm), tab_t)],
        out_specs=[tok] + k_specs + [tok, tok],
        out_shape=[tok_bf16] + k_shapes + [tok_f32, tok_bf16],
        compiler_params=_params("parallel", "parallel"),
        name="proj_attn",
    )(x, wq, wkt, wv, *tables, *tables_t)


def _proj_gla_kernel(x_ref, w_ref, wlr_ref, wa2_ref, ba_ref,
                     gq_ref, gk_ref, gv_ref, la_ref):
    xb = x_ref[...].astype(BF16)
    z = jnp.dot(xb, w_ref[...], preferred_element_type=F32)
    gq_ref[...] = z[:, :GLA_KEY]
    gk_ref[...] = z[:, GLA_KEY:2 * GLA_KEY]
    gv_ref[...] = z[:, 2 * GLA_KEY:]
    g_lr = jnp.dot(xb, wlr_ref[...], preferred_element_type=F32)
    logit = jnp.dot(g_lr.astype(BF16), wa2_ref[...], preferred_element_type=F32) + ba_ref[...]
    log_sig = jnp.minimum(logit, 0.0) - jnp.log1p(jnp.exp(-jnp.abs(logit)))
    la_ref[...] = log_sig / GLA_TAU


def _proj_gla(x, w_g, w_lr, w_a2, b_a, tm):
    n = x.shape[0]
    row = lambda i: (i, 0)
    const = lambda i: (0, 0)
    return pl.pallas_call(
        _proj_gla_kernel,
        grid=(n // tm,),
        in_specs=[pl.BlockSpec((tm, D_MODEL), row),
                  pl.BlockSpec((D_MODEL, 2 * D_MODEL), const),
                  pl.BlockSpec((D_MODEL, LANES), const),
                  pl.BlockSpec((LANES, GLA_KEY), const),
                  pl.BlockSpec((1, GLA_KEY), const)],
        out_specs=[pl.BlockSpec((tm, GLA_KEY), row), pl.BlockSpec((tm, GLA_KEY), row),
                   pl.BlockSpec((tm, D_MODEL), row), pl.BlockSpec((tm, GLA_KEY), row)],
        out_shape=[jax.ShapeDtypeStruct((n, GLA_KEY), F32),
                   jax.ShapeDtypeStruct((n, GLA_KEY), F32),
                   jax.ShapeDtypeStruct((n, D_MODEL), F32),
                   jax.ShapeDtypeStruct((n, GLA_KEY), F32)],
        compiler_params=_params("parallel"),
        name="proj_gla",
    )(x, w_g, w_lr, w_a2, b_a)


def _proj_gates_kernel(x_ref, w_ref, ga_ref, gb_ref):
    xb = x_ref[...].astype(BF16)
    z = jnp.dot(xb, w_ref[...], preferred_element_type=F32)
    dg, ma = z[:, :D_MODEL], z[:, D_MODEL:2 * D_MODEL]
    gg, mb = z[:, 2 * D_MODEL:3 * D_MODEL], z[:, 3 * D_MODEL:]
    ga_ref[...] = _sigmoid(ma) * (dg * _sigmoid(dg))
    gb_ref[...] = _sigmoid(mb) * (gg * _sigmoid(gg))


def _proj_gates(x, w_gates, tm):
    n = x.shape[0]
    row = lambda i: (i, 0)
    return pl.pallas_call(
        _proj_gates_kernel,
        grid=(n // tm,),
        in_specs=[pl.BlockSpec((tm, D_MODEL), row),
                  pl.BlockSpec((D_MODEL, 4 * D_MODEL), lambda i: (0, 0))],
        out_specs=[pl.BlockSpec((tm, D_MODEL), row)] * 2,
        out_shape=[jax.ShapeDtypeStruct((n, D_MODEL), F32)] * 2,
        compiler_params=_params("parallel"),
        name="proj_gates",
    )(x, w_gates)


def _lambda(lam_ref, lam_init):
    lp = lam_ref[...]
    a = jnp.sum(lp[0:1] * lp[1:2], axis=1, keepdims=True)
    b = jnp.sum(lp[2:3] * lp[3:4], axis=1, keepdims=True)
    return jnp.exp(a) - jnp.exp(b) + lam_init


def _head_norm(o, w, gain):
    return o * lax.rsqrt(jnp.mean(o * o, axis=-1, keepdims=True) + NORM_EPS) * w * gain


def _attn_prompt_kernel(q_ref, k_ref, v_ref, ga_ref, nw_ref, lam_ref, o_ref,
                        qq_scr, m_scr, l_scr, acc_scr, *, tq, tk, lam_init):
    i = pl.program_id(2)
    q = q_ref[0]
    lane = lax.broadcasted_iota(jnp.int32, (tq, LANES), 1)
    zero = jnp.zeros_like(q)
    qq_scr[:tq, :] = jnp.where(lane < DA_DK, q, zero)
    qq_scr[tq:, :] = jnp.where(lane >= DA_DK, q, zero)
    m_scr[...] = jnp.full(m_scr.shape, -jnp.inf, F32)
    l_scr[...] = jnp.zeros(l_scr.shape, F32)
    acc_scr[...] = jnp.zeros(acc_scr.shape, F32)

    def step(j, masked):
        k = k_ref[0, j]
        v = v_ref[0, pl.ds(pl.multiple_of(j * tk, tk), tk), :]
        s = jnp.dot(qq_scr[...], k, preferred_element_type=F32)
        if masked:
            r = lax.broadcasted_iota(jnp.int32, (2 * tq, tk), 0)
            c = lax.broadcasted_iota(jnp.int32, (2 * tq, tk), 1)
            qpos = i * tq + jnp.where(r >= tq, r - tq, r)
            s = jnp.where(j * tk + c <= qpos, s, MASK_VALUE)
        m_old = m_scr[...]
        m_new = jnp.maximum(m_old, jnp.max(s, axis=1, keepdims=True))
        alpha = jnp.exp(m_old - m_new)
        p = jnp.exp(s - m_new)
        l_scr[...] = alpha * l_scr[...] + jnp.sum(p, axis=1, keepdims=True)
        acc_scr[...] = alpha * acc_scr[...] + jnp.dot(p.astype(BF16), v, preferred_element_type=F32)
        m_scr[...] = m_new

    ratio = tq // tk
    lax.fori_loop(0, i * ratio, lambda j, c: (step(j, False), c)[1], 0)
    for d in range(ratio):
        step(i * ratio + d, True)

    lam = _lambda(lam_ref, lam_init)
    o0 = acc_scr[:tq, :] / l_scr[:tq, :]
    o1 = acc_scr[tq:, :] / l_scr[tq:, :]
    o = _head_norm(o0 - lam * o1, nw_ref[...], 1.0 - lam_init)
    o_ref[0] = o * ga_ref[0]


def _attn_prompt(q, kt, v, ga, norm_w, lam_p, lam_init, tq):
    b, t, _ = q.shape
    nk, tk = kt.shape[1], kt.shape[3]
    blk = pl.BlockSpec((1, tq, LANES), lambda bi, h, i: (bi, i, h))
    return pl.pallas_call(
        functools.partial(_attn_prompt_kernel, tq=tq, tk=tk, lam_init=lam_init),
        grid=(b, DA_HEADS, t // tq),
        in_specs=[blk,
                  pl.BlockSpec((1, nk, LANES, tk), lambda bi, h, i: (bi, 0, h, 0)),
                  pl.BlockSpec((1, t, LANES), lambda bi, h, i: (bi, 0, h)),
                  blk,
                  pl.BlockSpec((1, LANES), lambda bi, h, i: (0, 0)),
                  pl.BlockSpec((4, DA_DK), lambda bi, h, i: (0, 0))],
        out_specs=blk,
        out_shape=jax.ShapeDtypeStruct((b, t, D_MODEL), F32),
        scratch_shapes=[pltpu.VMEM((2 * tq, LANES), BF16),
                        pltpu.VMEM((2 * tq, 1), F32),
                        pltpu.VMEM((2 * tq, 1), F32),
                        pltpu.VMEM((2 * tq, LANES), F32)],
        compiler_params=_params("parallel", "parallel", "arbitrary"),
        name="attn_prompt",
    )(q, kt, v, ga, norm_w, lam_p)


def _attn_sample_kernel(pt_ref, q_ref, kn_ref, vn_ref, ga_ref, nw_ref, lam_ref, *rest,
                        lam_init, t_new):
    k_pages = rest[:N_PAGES]
    v_pages = rest[N_PAGES:2 * N_PAGES]
    o_ref, kb_scr, vb_scr = rest[2 * N_PAGES:]
    n_rows = 2 * DA_HEADS * t_new
    past = N_PAGES * PAGE_SIZE

    for p in range(N_PAGES):
        kb_scr[:, p * PAGE_SIZE:(p + 1) * PAGE_SIZE] = k_pages[p][...].astype(BF16)
        for h in range(DA_HEADS):
            vh = v_pages[p][pl.ds(h, PAGE_SIZE, stride=DA_HEADS), :]
            vb_scr[p * PAGE_SIZE:(p + 1) * PAGE_SIZE, h * DA_DV:(h + 1) * DA_DV] = vh.astype(BF16)
    pad = jnp.zeros((PAGE_SIZE - t_new, D_MODEL), F32)
    k_new = jnp.concatenate([kn_ref[0], pad], axis=0).astype(BF16)
    vb_scr[past:, :] = jnp.concatenate([vn_ref[0], pad], axis=0).astype(BF16)

    qf = q_ref[0].astype(F32)
    q_rep = jnp.concatenate([qf] * (2 * DA_HEADS), axis=0)
    r = lax.broadcasted_iota(jnp.int32, (n_rows, D_MODEL), 0)
    c = lax.broadcasted_iota(jnp.int32, (n_rows, D_MODEL), 1)
    q_exp = jnp.where(c // DA_DK == r // t_new, q_rep, 0.0).astype(BF16)

    s_past = jnp.dot(q_exp, kb_scr[...], preferred_element_type=F32)
    s_new = lax.dot_general(q_exp, k_new, NT_DIMS, preferred_element_type=F32)
    rr = lax.broadcasted_iota(jnp.int32, s_new.shape, 0)
    cc = lax.broadcasted_iota(jnp.int32, s_new.shape, 1)
    s_new = jnp.where(cc <= rr % t_new, s_new, MASK_VALUE)
    m = jnp.maximum(jnp.max(s_past, axis=1, keepdims=True), jnp.max(s_new, axis=1, keepdims=True))
    p_past = jnp.exp(s_past - m)
    p_new = jnp.exp(s_new - m)
    l = jnp.sum(p_past, axis=1, keepdims=True) + jnp.sum(p_new, axis=1, keepdims=True)
    p = jnp.concatenate([p_past, p_new], axis=1).astype(BF16)
    out = jnp.dot(p, vb_scr[...], preferred_element_type=F32) / l

    lam = _lambda(lam_ref, lam_init)
    heads = []
    for h in range(DA_HEADS):
        cols = slice(h * DA_DV, (h + 1) * DA_DV)
        o0 = out[(2 * h) * t_new:(2 * h + 1) * t_new, cols]
        o1 = out[(2 * h + 1) * t_new:(2 * h + 2) * t_new, cols]
        heads.append(_head_norm(o0 - lam * o1, nw_ref[...], 1.0 - lam_init))
    o_ref[0] = jnp.concatenate(heads, axis=1) * ga_ref[0]


def _attn_sample(page_table, q, k_new, v_new, ga, norm_w, lam_p, cache_kt, cache_v, layer, lam_init):
    nb, t_new, _ = q.shape
    tok = pl.BlockSpec((1, t_new, D_MODEL), lambda bi, pt: (bi, 0, 0))

    def page_spec(p):
        return pl.BlockSpec((None, None, D_MODEL, PAGE_SIZE),
                            lambda bi, pt: (layer, pt[bi * N_PAGES + p], 0, 0))

    pages = [page_spec(p) for p in range(N_PAGES)]
    return pl.pallas_call(
        functools.partial(_attn_sample_kernel, lam_init=lam_init, t_new=t_new),
        grid_spec=pltpu.PrefetchScalarGridSpec(
            num_scalar_prefetch=1,
            grid=(nb,),
            in_specs=[tok, tok, tok, tok,
                      pl.BlockSpec((1, LANES), lambda bi, pt: (0, 0)),
                      pl.BlockSpec((4, DA_DK), lambda bi, pt: (0, 0))] + pages + pages,
            out_specs=tok,
            scratch_shapes=[pltpu.VMEM((D_MODEL, N_PAGES * PAGE_SIZE), BF16),
                            pltpu.VMEM(((N_PAGES + 1) * PAGE_SIZE, D_MODEL), BF16)]),
        out_shape=jax.ShapeDtypeStruct((nb, t_new, D_MODEL), F32),
        compiler_params=_params("arbitrary"),
        name="attn_sample",
    )(page_table.reshape(-1), q, k_new, v_new, ga, norm_w, lam_p,
      *([cache_kt] * N_PAGES), *([cache_v] * N_PAGES))


def _split3(x):
    hi = x.astype(BF16)
    r1 = x - hi.astype(F32)
    mid = r1.astype(BF16)
    lo = (r1 - mid.astype(F32)).astype(BF16)
    return hi, mid, lo


def _gla_kernel(q_ref, k_ref, v_ref, la_ref, gb_ref, nw_ref, s0_ref, y_ref, sf_ref, s_scr,
                *, t_blk, rows, chunk):
    blk = pl.program_id(2)
    n_chunks = rows // chunk

    @pl.when(blk == 0)
    def _():
        s_scr[...] = s0_ref[0, 0]

    def load(ref, width):
        x = ref[0]
        if t_blk < rows:
            x = jnp.concatenate([x, jnp.zeros((rows - t_blk, width), F32)], axis=0)
        return x

    q, k, la = load(q_ref, GLA_DK), load(k_ref, GLA_DK), load(la_ref, GLA_DK)
    v = load(v_ref, GLA_DV)

    r = lax.broadcasted_iota(jnp.int32, (rows, rows), 0)
    c = lax.broadcasted_iota(jnp.int32, (rows, rows), 1)
    tri = (r // chunk == c // chunk) & (c <= r)
    tri_b = tri.astype(BF16)
    b = sum(jnp.dot(tri_b, part, preferred_element_type=F32) for part in _split3(la))

    q_dec = (q * jnp.exp(b) * (GLA_DK ** -0.5)).astype(BF16)
    k_dec = (k * jnp.exp(-b)).astype(BF16)
    a = lax.dot_general(q_dec, k_dec, NT_DIMS, preferred_element_type=F32)
    a = jnp.where(tri, a, 0.0)
    vb = v.astype(BF16)
    o_intra = jnp.dot(a.astype(BF16), vb, preferred_element_type=F32)

    b3 = b.reshape(n_chunks, chunk, GLA_DK)
    b_last = b3[:, chunk - 1:chunk, :]
    k_end = (k.reshape(n_chunks, chunk, GLA_DK) * jnp.exp(b_last - b3)).reshape(rows, GLA_DK).astype(BF16)
    decay = jnp.exp(b_last.reshape(n_chunks, GLA_DK))
    if n_chunks == 1:
        decay = jnp.broadcast_to(decay, (SUBLANES, GLA_DK))
    decay_t = decay.T

    s = s_scr[...]
    outs = []
    for ci in range(n_chunks):
        sl = slice(ci * chunk, (ci + 1) * chunk)
        outs.append(jnp.dot(q_dec[sl], s.astype(BF16), preferred_element_type=F32))
        ds = lax.dot_general(k_end[sl], vb[sl], TN_DIMS, preferred_element_type=F32)
        s = decay_t[:, ci:ci + 1] * s + ds
    s_scr[...] = s
    o = o_intra + jnp.concatenate(outs, axis=0)

    y = o * lax.rsqrt(jnp.mean(o * o, axis=-1, keepdims=True) + NORM_EPS) * nw_ref[...]
    y_ref[0] = y[:t_blk] * gb_ref[0]

    @pl.when(blk == pl.num_programs(2) - 1)
    def _():
        sf_ref[0, 0] = s


def _gla(gq, gk, gv, la, gb, norm_w, s0, t_blk):
    b, t, _ = gq.shape
    rows = max(t_blk, GLA_CHUNK)
    assert rows // GLA_CHUNK in (1, SUBLANES)
    kspec = pl.BlockSpec((1, t_blk, GLA_DK), lambda bi, h, i: (bi, i, h))
    vspec = pl.BlockSpec((1, t_blk, GLA_DV), lambda bi, h, i: (bi, i, h))
    sspec = pl.BlockSpec((1, 1, GLA_DK, GLA_DV), lambda bi, h, i: (bi, h, 0, 0))
    return pl.pallas_call(
        functools.partial(_gla_kernel, t_blk=t_blk, rows=rows, chunk=GLA_CHUNK),
        grid=(b, GLA_HEADS, t // t_blk),
        in_specs=[kspec, kspec, vspec, kspec, vspec,
                  pl.BlockSpec((1, GLA_DV), lambda bi, h, i: (0, 0)), sspec],
        out_specs=[vspec, sspec],
        out_shape=[jax.ShapeDtypeStruct((b, t, D_MODEL), F32),
                   jax.ShapeDtypeStruct(s0.shape, F32)],
        scratch_shapes=[pltpu.VMEM((GLA_DK, GLA_DV), F32)],
        compiler_params=_params("parallel", "parallel", "arbitrary"),
        name="gla",
    )(gq, gk, gv, la, gb, norm_w, s0)


def _out_proj_kernel(ya_ref, yb_ref, x_ref, w_ref, lw_ref, lb_ref, o_ref):
    merged = (ya_ref[...] + yb_ref[...]).astype(BF16)
    h = jnp.dot(merged, w_ref[...], preferred_element_type=F32)
    y = DEEPNORM_ALPHA * x_ref[...] + h
    mu = jnp.mean(y, axis=-1, keepdims=True)
    d = y - mu
    var = jnp.mean(d * d, axis=-1, keepdims=True)
    o_ref[...] = d * lax.rsqrt(var + NORM_EPS) * lw_ref[...] + lb_ref[...]


def _out_proj(ya, yb, x, w_out, ln_w, ln_b, tm):
    n = x.shape[0]
    row = lambda i: (i, 0)
    const = lambda i: (0, 0)
    return pl.pallas_call(
        _out_proj_kernel,
        grid=(n // tm,),
        in_specs=[pl.BlockSpec((tm, D_MODEL), row)] * 3
                 + [pl.BlockSpec((D_MODEL, D_MODEL), const),
                    pl.BlockSpec((1, D_MODEL), const), pl.BlockSpec((1, D_MODEL), const)],
        out_specs=pl.BlockSpec((tm, D_MODEL), row),
        out_shape=jax.ShapeDtypeStruct((n, D_MODEL), F32),
        compiler_params=_params("parallel"),
        name="out_proj",
    )(ya, yb, x, w_out, ln_w, ln_b)


def _rope_tables(pos):
    half = DA_DK // 2
    inv = ROPE_THETA ** (-jnp.arange(0, DA_DK, 2, dtype=F32) / DA_DK)
    ang = pos.astype(F32)[:, None] * inv[None, :]
    cos32, sin32 = jnp.cos(ang), jnp.sin(ang)
    cos = jnp.tile(cos32, (1, LANES // half))
    sin = jnp.tile(sin32, (1, LANES // half))
    first = (jnp.arange(LANES) % DA_DK) < half
    return (cos, jnp.where(first, -sin, 0.0), jnp.where(first, 0.0, sin)), (cos32.T, sin32.T)


def _layer_weights(layer, w_in, gla_w_a2, gla_b_a, w_out):
    w = w_in[layer]
    o = [0]
    for s in (1024, 1024, 1024, 1024, 512, 512, 1024, 1024, 16, 1024, 1024):
        o.append(o[-1] + s)
    seg = lambda i: w[:, o[i]:o[i + 1]]
    return dict(
        wq=seg(0).astype(BF16), wkt=seg(1).T.astype(BF16), wv=seg(2).astype(BF16),
        w_g=jnp.concatenate([seg(4), seg(5), seg(6)], axis=1).astype(BF16),
        w_gates=jnp.concatenate([seg(3), seg(9), seg(7), seg(10)], axis=1).astype(BF16),
        w_lr=jnp.pad(seg(8), ((0, 0), (0, LANES - GLA_RANK))).astype(BF16),
        w_a2=jnp.pad(gla_w_a2[layer], ((0, LANES - GLA_RANK), (0, 0))).astype(BF16),
        b_a=gla_b_a[layer][None, :],
        w_out=w_out[layer].astype(BF16))


def kernel(x_prompt, x_sample, cache_k, cache_v, state_gla, page_table, w_in,
           lam_q1, lam_k1, lam_q2, lam_k2, da_norm_w, gla_w_a2, gla_b_a, gla_norm_w,
           w_out, ln_w, ln_b):
    bp, tp, _ = x_prompt.shape
    bs, ts, _ = x_sample.shape
    ns = bs * ts
    tm = 512
    tab_p, tabt_p = _rope_tables(jnp.arange(tp))
    tab_s, tabt_s = _rope_tables(PAST_LEN + jnp.arange(ts))
    tab_s = tuple(jnp.tile(t, (tm // ts, 1)) for t in tab_s)
    tabt_s = tuple(jnp.tile(t, (1, tm // ts)) for t in tabt_s)
    xp = x_prompt.reshape(bp * tp, D_MODEL)
    xs = x_sample.reshape(ns, D_MODEL)
    s0_p = jnp.zeros((bp, GLA_HEADS, GLA_DK, GLA_DV), F32)
    n_pool = cache_k.shape[1]
    cache_kt = jnp.transpose(cache_k, (0, 1, 3, 4, 5, 2)).reshape(DEPTH, n_pool, D_MODEL, PAGE_SIZE)
    cache_v2 = cache_v.reshape(DEPTH, n_pool, PAGE_SIZE * DA_HEADS, DA_DV)

    kp_rows, vp_rows, sp_rows, ks_rows, vs_rows, ss_rows = [], [], [], [], [], []
    for layer in range(DEPTH):
        lam_init = 0.8 - 0.6 * math.exp(-0.3 * layer)
        w = _layer_weights(layer, w_in, gla_w_a2, gla_b_a, w_out)
        lam_p = jnp.stack([lam_q1[layer], lam_k1[layer], lam_q2[layer], lam_k2[layer]])
        da_w = da_norm_w[layer][None, :]
        gla_w = gla_norm_w[layer][None, :]
        lw, lb = ln_w[layer][None, :], ln_b[layer][None, :]
        sh3 = lambda a, b: a.reshape(b, -1, a.shape[-1])

        q, kt, ktb, vf, vb = _proj_attn(sh3(xp, bp), w["wq"], w["wkt"], w["wv"], tab_p, tabt_p, tm, True)
        gq, gk, gv, la = _proj_gla(xp, w["w_g"], w["w_lr"], w["w_a2"], w["b_a"], tm)
        ga, gb = _proj_gates(xp, w["w_gates"], tm)
        ya = _attn_prompt(q, ktb, vb, sh3(ga, bp), da_w, lam_p, lam_init, tq=tm)
        yb, s_p = _gla(sh3(gq, bp), sh3(gk, bp), sh3(gv, bp), sh3(la, bp), sh3(gb, bp), gla_w, s0_p, 256)
        xp = _out_proj(ya.reshape(-1, D_MODEL), yb.reshape(-1, D_MODEL), xp, w["w_out"], lw, lb, tm)
        kp_rows.append(jnp.transpose(kt.reshape(bp, DA_HEADS, 2, DA_DK, tp), (0, 4, 1, 2, 3)))
        vp_rows.append(vf.reshape(bp, tp, DA_HEADS, DA_DV))
        sp_rows.append(s_p)

        q, kf, vf, vb = _proj_attn(sh3(xs, 1), w["wq"], w["wkt"], w["wv"], tab_s, tabt_s, tm, False)
        gq, gk, gv, la = _proj_gla(xs, w["w_g"], w["w_lr"], w["w_a2"], w["b_a"], tm)
        ga, gb = _proj_gates(xs, w["w_gates"], tm)
        ya = _attn_sample(page_table, sh3(q, bs), sh3(kf, bs), sh3(vf, bs), sh3(ga, bs), da_w, lam_p,
                          cache_kt, cache_v2, layer, lam_init)
        yb, s_s = _gla(sh3(gq, bs), sh3(gk, bs), sh3(gv, bs), sh3(la, bs), sh3(gb, bs), gla_w,
                       state_gla[layer], ts)
        xs = _out_proj(ya.reshape(-1, D_MODEL), yb.reshape(-1, D_MODEL), xs, w["w_out"], lw, lb, tm)
        ks_rows.append(kf.reshape(bs, ts, DA_HEADS, 2, DA_DK))
        vs_rows.append(vf.reshape(bs, ts, DA_HEADS, DA_DV))
        ss_rows.append(s_s)

    return (xp.reshape(bp, tp, D_MODEL), xs.reshape(bs, ts, D_MODEL),
            jnp.stack(kp_rows), jnp.stack(vp_rows), jnp.stack(sp_rows),
            jnp.stack(ks_rows), jnp.stack(vs_rows), jnp.stack(ss_rows))
```

```python
import functools
import math

import jax
import jax.numpy as jnp
from jax import lax
from jax.experimental import pallas as pl
from jax.experimental.pallas import tpu as pltpu

D_MODEL = 1024
DEPTH = 2
PAST_LEN = 2048
PAGE_SIZE = 128
N_PAGES = PAST_LEN // PAGE_SIZE
DA_HEADS = 8
DA_DK = 64
DA_DV = 128
ROPE_THETA = 10000.0
GLA_HEADS = 4
GLA_KEY = 512
GLA_DK = 128
GLA_DV = 256
GLA_RANK = 16
GLA_TAU = 16.0
GLA_CHUNK = 32
NORM_EPS = 1e-5
DEEPNORM_ALPHA = (2.0 * DEPTH) ** 0.25
MASK_VALUE = -1e30
LOG2_E = math.log2(math.e)

LANES = 128
SUBLANES = 8
VMEM_LIMIT = 56 * 1024 * 1024

F32 = jnp.float32
BF16 = jnp.bfloat16
NT_DIMS = (((1,), (1,)), ((), ()))
TN_DIMS = (((0,), (0,)), ((), ()))


def _params(*sem):
    return pltpu.CompilerParams(dimension_semantics=sem, vmem_limit_bytes=VMEM_LIMIT)


def _sigmoid(x):
    return 1.0 / (1.0 + jnp.exp(-x))


def _rope_lanes(zj, cos, sina, sinb):
    return zj * cos + pltpu.roll(zj, LANES - 32, 1) * sina + pltpu.roll(zj, 32, 1) * sinb


def _rope_rows(zt, cost, sint):
    half = DA_DK // 2
    z4 = zt.reshape(2 * DA_HEADS, 2, half, zt.shape[1])
    x1, x2 = z4[:, 0], z4[:, 1]
    ct, st = cost[None], sint[None]
    return jnp.stack([x1 * ct - x2 * st, x2 * ct + x1 * st], axis=1).reshape(zt.shape)


def _rope_rows_token_major(z, cos, sina, sinb, store):
    for j in range(D_MODEL // LANES):
        sl = slice(j * LANES, (j + 1) * LANES)
        store(sl, _rope_lanes(z[:, sl], cos, sina, sinb))


def _proj_attn_prompt_kernel(x_ref, wqt_ref, wkt_ref, wvt_ref, cos_ref, sina_ref, sinb_ref, cost_ref, sint_ref,
                             *rest):
    qt_ref, kt_ref, kb_ref, vf_ref, vtb_ref = rest[-5:]
    xb = x_ref[0].astype(BF16)
    cost, sint = cost_ref[...], sint_ref[...]
    tn = lambda w_ref: lax.dot_general(w_ref[...], xb, NT_DIMS, preferred_element_type=F32)
    nt = lambda w_ref: lax.dot_general(xb, w_ref[...], NT_DIMS, preferred_element_type=F32)
    qt_ref[0] = (_rope_rows(tn(wqt_ref), cost, sint) * (DA_DK ** -0.5 * LOG2_E)).astype(BF16)
    kt_ref[0] = _rope_rows(tn(wkt_ref), cost, sint)

    def store_k(sl, val):
        kb_ref[0, :, sl] = val.astype(BF16)
    _rope_rows_token_major(nt(wkt_ref), cos_ref[...], sina_ref[...], sinb_ref[...], store_k)
    vf_ref[0] = nt(wvt_ref)
    vtb_ref[0, 0] = tn(wvt_ref).astype(BF16)


def _proj_attn_sample_kernel(x_ref, wqt_ref, wkt_ref, wvt_ref, cos_ref, sina_ref, sinb_ref,
                             q_ref, k_ref, v_ref):
    xb = x_ref[0].astype(BF16)
    cos, sina, sinb = cos_ref[...], sina_ref[...], sinb_ref[...]
    nt = lambda w_ref: lax.dot_general(xb, w_ref[...], NT_DIMS, preferred_element_type=F32)

    def store_q(sl, val):
        q_ref[0, :, sl] = (val * (DA_DK ** -0.5)).astype(BF16)

    def store_k(sl, val):
        k_ref[0, :, sl] = val
    _rope_rows_token_major(nt(wqt_ref), cos, sina, sinb, store_q)
    _rope_rows_token_major(nt(wkt_ref), cos, sina, sinb, store_k)
    v_ref[0] = nt(wvt_ref)


def _proj_attn(x, wqt, wkt, wvt, tables, tables_t, tm, prompt, layer=0, prev_kv=None):
    b, t, _ = x.shape
    nt = tables[0].shape[0] // tm
    tab = lambda bi, i: (i % nt, 0)
    tab_t = lambda bi, i: (0, i % nt)
    wspec = pl.BlockSpec((D_MODEL, D_MODEL), lambda bi, i: (0, 0))
    tok = pl.BlockSpec((1, tm, D_MODEL), lambda bi, i: (bi, i, 0))
    tok_t = pl.BlockSpec((1, D_MODEL, tm), lambda bi, i: (bi, 0, i))
    tabs = [pl.BlockSpec((tm, LANES), tab)] * 3
    shape = lambda s, dt: jax.ShapeDtypeStruct(s, dt)
    aliases = {}
    if prompt:
        body = _proj_attn_prompt_kernel
        in_specs = [tok, wspec, wspec, wspec] + tabs + [pl.BlockSpec((DA_DK // 2, tm), tab_t)] * 2
        args = [x, wqt, wkt, wvt, *tables, *tables_t]
        if prev_kv is not None:
            aliases = {len(args): 1, len(args) + 1: 3}
            in_specs += [pl.BlockSpec(memory_space=pl.ANY)] * 2
            args += list(prev_kv)
        out_specs = [tok_t,
                     pl.BlockSpec((None, 1, D_MODEL, tm), lambda bi, i: (layer, bi, 0, i)),
                     tok,
                     pl.BlockSpec((None, 1, tm, D_MODEL), lambda bi, i: (layer, bi, i, 0)),
                     pl.BlockSpec((1, 1, D_MODEL, tm), lambda bi, i: (bi, i, 0, 0))]
        out_shape = [shape((b, D_MODEL, t), BF16), shape((DEPTH, b, D_MODEL, t), F32),
                     shape((b, t, D_MODEL), BF16), shape((DEPTH, b, t, D_MODEL), F32),
                     shape((b, t // tm, D_MODEL, tm), BF16)]
    else:
        body = _proj_attn_sample_kernel
        in_specs = [tok, wspec, wspec, wspec] + tabs
        args = [x, wqt, wkt, wvt, *tables]
        out_specs = [tok, tok, tok]
        out_shape = [shape((b, t, D_MODEL), BF16), shape((b, t, D_MODEL), F32), shape((b, t, D_MODEL), F32)]
    return pl.pallas_call(
        body, grid=(b, t // tm), in_specs=in_specs, out_specs=out_specs, out_shape=out_shape,
        input_output_aliases=aliases,
        compiler_params=_params("parallel", "parallel"),
        name="proj_attn_prompt" if prompt else "proj_attn_sample",
    )(*args)


def _proj_gla_kernel(x_ref, w_ref, wlr_ref, wa2_ref, ba_ref,
                     gq_ref, gk_ref, gv_ref, la_ref):
    xb = x_ref[...].astype(BF16)
    z = jnp.dot(xb, w_ref[...], preferred_element_type=F32)
    gq_ref[...] = z[:, :GLA_KEY]
    gk_ref[...] = z[:, GLA_KEY:2 * GLA_KEY]
    gv_ref[...] = z[:, 2 * GLA_KEY:]
    g_lr = jnp.dot(xb, wlr_ref[...], preferred_element_type=F32)
    logit = jnp.dot(g_lr.astype(BF16), wa2_ref[...], preferred_element_type=F32) + ba_ref[...]
    log_sig = jnp.minimum(logit, 0.0) - jnp.log1p(jnp.exp(-jnp.abs(logit)))
    la_ref[...] = log_sig / GLA_TAU


def _proj_gla(x, w_g, w_lr, w_a2, b_a, tm):
    n = x.shape[0]
    row = lambda i: (i, 0)
    const = lambda i: (0, 0)
    return pl.pallas_call(
        _proj_gla_kernel,
        grid=(n // tm,),
        in_specs=[pl.BlockSpec((tm, D_MODEL), row),
                  pl.BlockSpec((D_MODEL, 2 * D_MODEL), const),
                  pl.BlockSpec((D_MODEL, LANES), const),
                  pl.BlockSpec((LANES, GLA_KEY), const),
                  pl.BlockSpec((1, GLA_KEY), const)],
        out_specs=[pl.BlockSpec((tm, GLA_KEY), row), pl.BlockSpec((tm, GLA_KEY), row),
                   pl.BlockSpec((tm, D_MODEL), row), pl.BlockSpec((tm, GLA_KEY), row)],
        out_shape=[jax.ShapeDtypeStruct((n, GLA_KEY), F32),
                   jax.ShapeDtypeStruct((n, GLA_KEY), F32),
                   jax.ShapeDtypeStruct((n, D_MODEL), F32),
                   jax.ShapeDtypeStruct((n, GLA_KEY), F32)],
        compiler_params=_params("parallel"),
        name="proj_gla",
    )(x, w_g, w_lr, w_a2, b_a)


def _proj_gates_kernel(x_ref, w_ref, ga_ref, gb_ref):
    xb = x_ref[...].astype(BF16)
    z = jnp.dot(xb, w_ref[...], preferred_element_type=F32)
    dg, ma = z[:, :D_MODEL], z[:, D_MODEL:2 * D_MODEL]
    gg, mb = z[:, 2 * D_MODEL:3 * D_MODEL], z[:, 3 * D_MODEL:]
    ga_ref[...] = _sigmoid(ma) * (dg * _sigmoid(dg))
    gb_ref[...] = _sigmoid(mb) * (gg * _sigmoid(gg))


def _proj_gates(x, w_gates, tm):
    n = x.shape[0]
    row = lambda i: (i, 0)
    return pl.pallas_call(
        _proj_gates_kernel,
        grid=(n // tm,),
        in_specs=[pl.BlockSpec((tm, D_MODEL), row),
                  pl.BlockSpec((D_MODEL, 4 * D_MODEL), lambda i: (0, 0))],
        out_specs=[pl.BlockSpec((tm, D_MODEL), row)] * 2,
        out_shape=[jax.ShapeDtypeStruct((n, D_MODEL), F32)] * 2,
        compiler_params=_params("parallel"),
        name="proj_gates",
    )(x, w_gates)


def _lambda(lam_ref, lam_init):
    lp = lam_ref[...]
    a = jnp.sum(lp[0:1] * lp[1:2], axis=1, keepdims=True)
    b = jnp.sum(lp[2:3] * lp[3:4], axis=1, keepdims=True)
    return jnp.exp(a) - jnp.exp(b) + lam_init


def _head_norm(o, w, gain):
    return o * lax.rsqrt(jnp.mean(o * o, axis=-1, keepdims=True) + NORM_EPS) * w * gain


def _attn_prompt_kernel(qt_ref, k_ref, vt_ref, ga_ref, nw_ref, lam_ref, o_ref,
                        qq_scr, sa_scr, sb_scr, m_scr, l_scr, acc_scr, *, tq, tk, lam_init):
    i = pl.program_id(2)
    qt = qt_ref[0]
    row = lax.broadcasted_iota(jnp.int32, (LANES, tq), 0)
    zero = jnp.zeros_like(qt)
    qq_scr[:, :tq] = jnp.where(row < DA_DK, qt, zero)
    qq_scr[:, tq:] = jnp.where(row >= DA_DK, qt, zero)
    m_scr[...] = jnp.full(m_scr.shape, -jnp.inf, F32)
    l_scr[...] = jnp.zeros(l_scr.shape, F32)
    acc_scr[...] = jnp.zeros(acc_scr.shape, F32)

    def scores(j, s_ref):
        k = k_ref[0, pl.ds(pl.multiple_of(j * tk, tk), tk), :]
        s_ref[...] = jnp.dot(k, qq_scr[...], preferred_element_type=F32)

    def softmax_pv(j, s_ref, masked):
        s = s_ref[...]
        if masked:
            r = lax.broadcasted_iota(jnp.int32, (tk, 2 * tq), 0)
            c = lax.broadcasted_iota(jnp.int32, (tk, 2 * tq), 1)
            qpos = i * tq + jnp.where(c >= tq, c - tq, c)
            s = jnp.where(j * tk + r <= qpos, s, MASK_VALUE)
        m_old = m_scr[...]
        m_new = jnp.maximum(m_old, jnp.max(s, axis=0, keepdims=True))
        alpha = jnp.exp2(m_old - m_new)
        p = jnp.exp2(s - m_new)
        l_scr[...] = alpha * l_scr[...] + jnp.sum(p, axis=0, keepdims=True)
        acc_scr[...] = alpha * acc_scr[...] + jnp.dot(vt_ref[0, j], p.astype(BF16),
                                                      preferred_element_type=F32)
        m_scr[...] = m_new

    assert tq == tk
    scores(0, sa_scr)

    def body(t, carry):
        scores(2 * t + 1, sb_scr)
        softmax_pv(2 * t, sa_scr, False)
        scores(2 * t + 2, sa_scr)
        softmax_pv(2 * t + 1, sb_scr, False)
        return carry
    lax.fori_loop(0, i // 2, body, 0)

    @pl.when(i % 2 == 1)
    def _():
        scores(i, sb_scr)
        softmax_pv(i - 1, sa_scr, False)
        softmax_pv(i, sb_scr, True)

    @pl.when(i % 2 == 0)
    def _():
        softmax_pv(i, sa_scr, True)

    lam = _lambda(lam_ref, lam_init)
    ot = acc_scr[...] / l_scr[...]
    o = (ot[:, :tq] - lam * ot[:, tq:]).T
    o_ref[0] = _head_norm(o, nw_ref[...], 1.0 - lam_init) * ga_ref[0]


def _attn_prompt(qt, k, vt, ga, norm_w, lam_p, lam_init, tq, tk):
    b, t, _ = k.shape
    nk = t // tk
    assert vt.shape == (b, nk, D_MODEL, tk)
    blk = pl.BlockSpec((1, tq, LANES), lambda bi, h, i: (bi, i, h))
    return pl.pallas_call(
        functools.partial(_attn_prompt_kernel, tq=tq, tk=tk, lam_init=lam_init),
        grid=(b, DA_HEADS, t // tq),
        in_specs=[pl.BlockSpec((1, LANES, tq), lambda bi, h, i: (bi, h, i)),
                  pl.BlockSpec((1, t, LANES), lambda bi, h, i: (bi, 0, h)),
                  pl.BlockSpec((1, nk, LANES, tk), lambda bi, h, i: (bi, 0, h, 0)),
                  blk,
                  pl.BlockSpec((1, LANES), lambda bi, h, i: (0, 0)),
                  pl.BlockSpec((4, DA_DK), lambda bi, h, i: (0, 0))],
        out_specs=blk,
        out_shape=jax.ShapeDtypeStruct((b, t, D_MODEL), F32),
        scratch_shapes=[pltpu.VMEM((LANES, 2 * tq), BF16),
                        pltpu.VMEM((tk, 2 * tq), F32),
                        pltpu.VMEM((tk, 2 * tq), F32),
                        pltpu.VMEM((1, 2 * tq), F32),
                        pltpu.VMEM((1, 2 * tq), F32),
                        pltpu.VMEM((LANES, 2 * tq), F32)],
        compiler_params=_params("parallel", "parallel", "arbitrary"),
        name="attn_prompt",
    )(qt, k, vt, ga, norm_w, lam_p)


def _attn_sample_kernel(pt_ref, q_ref, kn_ref, vn_ref, ga_ref, nw_ref, lam_ref, *rest,
                        lam_init, t_new):
    k_pages = rest[:N_PAGES]
    v_pages = rest[N_PAGES:2 * N_PAGES]
    o_ref, kb_scr, vb_scr = rest[2 * N_PAGES:]
    n_rows = 2 * DA_HEADS * t_new
    past = N_PAGES * PAGE_SIZE

    for p in range(N_PAGES):
        kb_scr[:, p * PAGE_SIZE:(p + 1) * PAGE_SIZE] = k_pages[p][...].astype(BF16)
        for h in range(DA_HEADS):
            vh = v_pages[p][pl.ds(h, PAGE_SIZE, stride=DA_HEADS), :]
            vb_scr[p * PAGE_SIZE:(p + 1) * PAGE_SIZE, h * DA_DV:(h + 1) * DA_DV] = vh.astype(BF16)
    pad = jnp.zeros((PAGE_SIZE - t_new, D_MODEL), F32)
    k_new = jnp.concatenate([kn_ref[0], pad], axis=0).astype(BF16)
    vb_scr[past:, :] = jnp.concatenate([vn_ref[0], pad], axis=0).astype(BF16)

    qf = q_ref[0].astype(F32)
    q_rep = jnp.concatenate([qf] * (2 * DA_HEADS), axis=0)
    r = lax.broadcasted_iota(jnp.int32, (n_rows, D_MODEL), 0)
    c = lax.broadcasted_iota(jnp.int32, (n_rows, D_MODEL), 1)
    q_exp = jnp.where(c // DA_DK == r // t_new, q_rep, 0.0).astype(BF16)

    s_past = jnp.dot(q_exp, kb_scr[...], preferred_element_type=F32)
    s_new = lax.dot_general(q_exp, k_new, NT_DIMS, preferred_element_type=F32)
    rr = lax.broadcasted_iota(jnp.int32, s_new.shape, 0)
    cc = lax.broadcasted_iota(jnp.int32, s_new.shape, 1)
    s_new = jnp.where(cc <= rr % t_new, s_new, MASK_VALUE)
    m = jnp.maximum(jnp.max(s_past, axis=1, keepdims=True), jnp.max(s_new, axis=1, keepdims=True))
    p_past = jnp.exp(s_past - m)
    p_new = jnp.exp(s_new - m)
    l = jnp.sum(p_past, axis=1, keepdims=True) + jnp.sum(p_new, axis=1, keepdims=True)
    p = jnp.concatenate([p_past, p_new], axis=1).astype(BF16)
    out = jnp.dot(p, vb_scr[...], preferred_element_type=F32) / l

    lam = _lambda(lam_ref, lam_init)
    heads = []
    for h in range(DA_HEADS):
        cols = slice(h * DA_DV, (h + 1) * DA_DV)
        o0 = out[(2 * h) * t_new:(2 * h + 1) * t_new, cols]
        o1 = out[(2 * h + 1) * t_new:(2 * h + 2) * t_new, cols]
        heads.append(_head_norm(o0 - lam * o1, nw_ref[...], 1.0 - lam_init))
    o_ref[0] = jnp.concatenate(heads, axis=1) * ga_ref[0]


def _attn_sample(page_table, q, k_new, v_new, ga, norm_w, lam_p, cache_kt, cache_v, layer, lam_init):
    nb, t_new, _ = q.shape
    tok = pl.BlockSpec((1, t_new, D_MODEL), lambda bi, pt: (bi, 0, 0))

    def page_spec(p):
        return pl.BlockSpec((None, None, D_MODEL, PAGE_SIZE),
                            lambda bi, pt: (layer, pt[bi * N_PAGES + p], 0, 0))

    pages = [page_spec(p) for p in range(N_PAGES)]
    return pl.pallas_call(
        functools.partial(_attn_sample_kernel, lam_init=lam_init, t_new=t_new),
        grid_spec=pltpu.PrefetchScalarGridSpec(
            num_scalar_prefetch=1,
            grid=(nb,),
            in_specs=[tok, tok, tok, tok,
                      pl.BlockSpec((1, LANES), lambda bi, pt: (0, 0)),
                      pl.BlockSpec((4, DA_DK), lambda bi, pt: (0, 0))] + pages + pages,
            out_specs=tok,
            scratch_shapes=[pltpu.VMEM((D_MODEL, N_PAGES * PAGE_SIZE), BF16),
                            pltpu.VMEM(((N_PAGES + 1) * PAGE_SIZE, D_MODEL), BF16)]),
        out_shape=jax.ShapeDtypeStruct((nb, t_new, D_MODEL), F32),
        compiler_params=_params("arbitrary"),
        name="attn_sample",
    )(page_table.reshape(-1), q, k_new, v_new, ga, norm_w, lam_p,
      *([cache_kt] * N_PAGES), *([cache_v] * N_PAGES))


def _split3(x):
    hi = x.astype(BF16)
    r1 = x - hi.astype(F32)
    mid = r1.astype(BF16)
    lo = (r1 - mid.astype(F32)).astype(BF16)
    return hi, mid, lo


def _gla_kernel(*refs, t_blk, rows, chunk, has_s0):
    q_ref, k_ref, v_ref, la_ref, gb_ref, nw_ref = refs[:6]
    y_ref, sf_ref, s_scr = refs[-3:]
    blk = pl.program_id(1)
    n_chunks = rows // chunk

    @pl.when(blk == 0)
    def _():
        if has_s0:
            s_scr[...] = refs[6][0]
        else:
            s_scr[...] = jnp.zeros(s_scr.shape, F32)

    def load(ref, lo, width):
        x = ref[0, :, lo:lo + width]
        if t_blk < rows:
            x = jnp.concatenate([x, jnp.zeros((rows - t_blk, width), F32)], axis=0)
        return x

    r = lax.broadcasted_iota(jnp.int32, (rows, rows), 0)
    c = lax.broadcasted_iota(jnp.int32, (rows, rows), 1)
    tri = (r // chunk == c // chunk) & (c <= r)
    tri_b = tri.astype(BF16)

    q_dec, k_end, vb, o_intra, decay_t, s = [], [], [], [], [], []
    for h in range(GLA_HEADS):
        q, k, la = (load(ref, h * GLA_DK, GLA_DK) for ref in (q_ref, k_ref, la_ref))
        v = load(v_ref, h * GLA_DV, GLA_DV)
        b = sum(jnp.dot(tri_b, part, preferred_element_type=F32) for part in _split3(la))
        qd = (q * jnp.exp(b) * (GLA_DK ** -0.5)).astype(BF16)
        k_dec = (k * jnp.exp(-b)).astype(BF16)
        a = lax.dot_general(qd, k_dec, NT_DIMS, preferred_element_type=F32)
        a = jnp.where(tri, a, 0.0)
        vb.append(v.astype(BF16))
        q_dec.append(qd)
        o_intra.append(jnp.dot(a.astype(BF16), vb[h], preferred_element_type=F32))
        b3 = b.reshape(n_chunks, chunk, GLA_DK)
        b_last = b3[:, chunk - 1:chunk, :]
        k_end.append((k.reshape(n_chunks, chunk, GLA_DK) * jnp.exp(b_last - b3))
                     .reshape(rows, GLA_DK).astype(BF16))
        decay = jnp.exp(b_last.reshape(n_chunks, GLA_DK))
        if n_chunks == 1:
            decay = jnp.broadcast_to(decay, (SUBLANES, GLA_DK))
        decay_t.append(decay.T)
        s.append(s_scr[h])

    outs = [[] for _ in range(GLA_HEADS)]
    for ci in range(n_chunks):
        sl = slice(ci * chunk, (ci + 1) * chunk)
        for h in range(GLA_HEADS):
            outs[h].append(jnp.dot(q_dec[h][sl], s[h].astype(BF16), preferred_element_type=F32))
            ds = lax.dot_general(k_end[h][sl], vb[h][sl], TN_DIMS, preferred_element_type=F32)
            s[h] = decay_t[h][:, ci:ci + 1] * s[h] + ds

    for h in range(GLA_HEADS):
        s_scr[h] = s[h]
        o = o_intra[h] + (jnp.concatenate(outs[h], axis=0) if n_chunks > 1 else outs[h][0])
        y = o * lax.rsqrt(jnp.mean(o * o, axis=-1, keepdims=True) + NORM_EPS) * nw_ref[...]
        cols = slice(h * GLA_DV, (h + 1) * GLA_DV)
        y_ref[0, :, cols] = y[:t_blk] * gb_ref[0, :, cols]

    @pl.when(blk == pl.num_programs(1) - 1)
    def _():
        for h in range(GLA_HEADS):
            sf_ref[0, h] = s[h]


def _gla(gq, gk, gv, la, gb, norm_w, state, layer, prev_states, t_blk):
    b, t, _ = gq.shape
    rows = max(t_blk, GLA_CHUNK)
    assert rows // GLA_CHUNK in (1, SUBLANES)
    kspec = pl.BlockSpec((1, t_blk, GLA_KEY), lambda bi, i: (bi, i, 0))
    vspec = pl.BlockSpec((1, t_blk, D_MODEL), lambda bi, i: (bi, i, 0))
    sspec = pl.BlockSpec((None, 1, GLA_HEADS, GLA_DK, GLA_DV), lambda bi, i: (layer, bi, 0, 0, 0))
    in_specs = [kspec, kspec, vspec, kspec, vspec, pl.BlockSpec((1, GLA_DV), lambda bi, i: (0, 0))]
    args = [gq, gk, gv, la, gb, norm_w]
    if state is not None:
        in_specs.append(sspec)
        args.append(state)
    aliases = {}
    if prev_states is not None:
        aliases = {len(args): 1}
        in_specs.append(pl.BlockSpec(memory_space=pl.ANY))
        args.append(prev_states)
    return pl.pallas_call(
        functools.partial(_gla_kernel, t_blk=t_blk, rows=rows, chunk=GLA_CHUNK, has_s0=state is not None),
        grid=(b, t // t_blk),
        in_specs=in_specs,
        out_specs=[vspec, sspec],
        out_shape=[jax.ShapeDtypeStruct((b, t, D_MODEL), F32),
                   jax.ShapeDtypeStruct((DEPTH, b, GLA_HEADS, GLA_DK, GLA_DV), F32)],
        scratch_shapes=[pltpu.VMEM((GLA_HEADS, GLA_DK, GLA_DV), F32)],
        input_output_aliases=aliases,
        compiler_params=_params("parallel", "arbitrary"),
        name="gla",
    )(*args)


def _out_proj_kernel(ya_ref, yb_ref, x_ref, w_ref, lw_ref, lb_ref, o_ref):
    merged = (ya_ref[...] + yb_ref[...]).astype(BF16)
    h = jnp.dot(merged, w_ref[...], preferred_element_type=F32)
    y = DEEPNORM_ALPHA * x_ref[...] + h
    mu = jnp.mean(y, axis=-1, keepdims=True)
    d = y - mu
    var = jnp.mean(d * d, axis=-1, keepdims=True)
    o_ref[...] = d * lax.rsqrt(var + NORM_EPS) * lw_ref[...] + lb_ref[...]


def _out_proj(ya, yb, x, w_out, ln_w, ln_b, tm):
    n = x.shape[0]
    row = lambda i: (i, 0)
    const = lambda i: (0, 0)
    return pl.pallas_call(
        _out_proj_kernel,
        grid=(n // tm,),
        in_specs=[pl.BlockSpec((tm, D_MODEL), row)] * 3
                 + [pl.BlockSpec((D_MODEL, D_MODEL), const),
                    pl.BlockSpec((1, D_MODEL), const), pl.BlockSpec((1, D_MODEL), const)],
        out_specs=pl.BlockSpec((tm, D_MODEL), row),
        out_shape=jax.ShapeDtypeStruct((n, D_MODEL), F32),
        compiler_params=_params("parallel"),
        name="out_proj",
    )(ya, yb, x, w_out, ln_w, ln_b)


def _rope_tables(pos):
    half = DA_DK // 2
    inv = ROPE_THETA ** (-jnp.arange(0, DA_DK, 2, dtype=F32) / DA_DK)
    ang = pos.astype(F32)[:, None] * inv[None, :]
    cos32, sin32 = jnp.cos(ang), jnp.sin(ang)
    cos = jnp.tile(cos32, (1, LANES // half))
    sin = jnp.tile(sin32, (1, LANES // half))
    first = (jnp.arange(LANES) % DA_DK) < half
    return (cos, jnp.where(first, -sin, 0.0), jnp.where(first, 0.0, sin)), (cos32.T, sin32.T)


def _layer_weights(layer, w_in, gla_w_a2, gla_b_a, w_out):
    w = w_in[layer]
    o = [0]
    for s in (1024, 1024, 1024, 1024, 512, 512, 1024, 1024, 16, 1024, 1024):
        o.append(o[-1] + s)
    seg = lambda i: w[:, o[i]:o[i + 1]]
    return dict(
        wqt=seg(0).T.astype(BF16), wkt=seg(1).T.astype(BF16), wvt=seg(2).T.astype(BF16),
        w_g=jnp.concatenate([seg(4), seg(5), seg(6)], axis=1).astype(BF16),
        w_gates=jnp.concatenate([seg(3), seg(9), seg(7), seg(10)], axis=1).astype(BF16),
        w_lr=jnp.pad(seg(8), ((0, 0), (0, LANES - GLA_RANK))).astype(BF16),
        w_a2=jnp.pad(gla_w_a2[layer], ((0, LANES - GLA_RANK), (0, 0))).astype(BF16),
        b_a=gla_b_a[layer][None, :],
        w_out=w_out[layer].astype(BF16))


def kernel(x_prompt, x_sample, cache_k, cache_v, state_gla, page_table, w_in,
           lam_q1, lam_k1, lam_q2, lam_k2, da_norm_w, gla_w_a2, gla_b_a, gla_norm_w,
           w_out, ln_w, ln_b):
    bp, tp, _ = x_prompt.shape
    bs, ts, _ = x_sample.shape
    ns = bs * ts
    tm = 512
    tab_p, tabt_p = _rope_tables(jnp.arange(tp))
    tab_s, _ = _rope_tables(PAST_LEN + jnp.arange(ts))
    tab_s = tuple(jnp.tile(t, (tm // ts, 1)) for t in tab_s)
    xp = x_prompt.reshape(bp * tp, D_MODEL)
    xs = x_sample.reshape(ns, D_MODEL)
    n_pool = cache_k.shape[1]
    cache_kt = jnp.transpose(cache_k, (0, 1, 3, 4, 5, 2)).reshape(DEPTH, n_pool, D_MODEL, PAGE_SIZE)
    cache_v2 = cache_v.reshape(DEPTH, n_pool, PAGE_SIZE * DA_HEADS, DA_DV)

    ks_rows, vs_rows = [], []
    kv_p = sp_all = ss_all = None
    for layer in range(DEPTH):
        lam_init = 0.8 - 0.6 * math.exp(-0.3 * layer)
        w = _layer_weights(layer, w_in, gla_w_a2, gla_b_a, w_out)
        lam_p = jnp.stack([lam_q1[layer], lam_k1[layer], lam_q2[layer], lam_k2[layer]])
        da_w = da_norm_w[layer][None, :]
        gla_w = gla_norm_w[layer][None, :]
        lw, lb = ln_w[layer][None, :], ln_b[layer][None, :]
        sh3 = lambda a, b: a.reshape(b, -1, a.shape[-1])

        qt, kt_all, kb, vf_all, vtb = _proj_attn(sh3(xp, bp), w["wqt"], w["wkt"], w["wvt"], tab_p, tabt_p, tm,
                                                 True, layer, kv_p)
        kv_p = (kt_all, vf_all)
        gq, gk, gv, la = _proj_gla(xp, w["w_g"], w["w_lr"], w["w_a2"], w["b_a"], tm)
        ga, gb = _proj_gates(xp, w["w_gates"], tm)
        ya = _attn_prompt(qt, kb, vtb, sh3(ga, bp), da_w, lam_p, lam_init, tq=tm, tk=tm)
        yb, sp_all = _gla(sh3(gq, bp), sh3(gk, bp), sh3(gv, bp), sh3(la, bp), sh3(gb, bp), gla_w,
                          None, layer, sp_all, 256)
        xp = _out_proj(ya.reshape(-1, D_MODEL), yb.reshape(-1, D_MODEL), xp, w["w_out"], lw, lb, tm)

        q, kf, vf = _proj_attn(sh3(xs, 1), w["wqt"], w["wkt"], w["wvt"], tab_s, None, tm, False)
        gq, gk, gv, la = _proj_gla(xs, w["w_g"], w["w_lr"], w["w_a2"], w["b_a"], tm)
        ga, gb = _proj_gates(xs, w["w_gates"], tm)
        ya = _attn_sample(page_table, sh3(q, bs), sh3(kf, bs), sh3(vf, bs), sh3(ga, bs), da_w, lam_p,
                          cache_kt, cache_v2, layer, lam_init)
        yb, ss_all = _gla(sh3(gq, bs), sh3(gk, bs), sh3(gv, bs), sh3(la, bs), sh3(gb, bs), gla_w,
                          state_gla, layer, ss_all, ts)
        xs = _out_proj(ya.reshape(-1, D_MODEL), yb.reshape(-1, D_MODEL), xs, w["w_out"], lw, lb, tm)
        ks_rows.append(kf.reshape(bs, ts, DA_HEADS, 2, DA_DK))
        vs_rows.append(vf.reshape(bs, ts, DA_HEADS, DA_DV))

    kt_all, vf_all = kv_p
    new_k_p = jnp.transpose(kt_all.reshape(DEPTH, bp, DA_HEADS, 2, DA_DK, tp), (0, 1, 5, 2, 3, 4))
    return (xp.reshape(bp, tp, D_MODEL), xs.reshape(bs, ts, D_MODEL),
            new_k_p, vf_all.reshape(DEPTH, bp, tp, DA_HEADS, DA_DV), sp_all,
            jnp.stack(ks_rows), jnp.stack(vs_rows), ss_all)
```

```python
import functools
import math

import jax
import jax.numpy as jnp
from jax import lax
from jax.experimental import pallas as pl
from jax.experimental.pallas import tpu as pltpu

D_MODEL = 1024
DEPTH = 2
PAST_LEN = 2048
PAGE_SIZE = 128
N_PAGES = PAST_LEN // PAGE_SIZE
DA_HEADS = 8
DA_DK = 64
DA_DV = 128
ROPE_THETA = 10000.0
GLA_HEADS = 4
GLA_KEY = 512
GLA_DK = 128
GLA_DV = 256
GLA_RANK = 16
GLA_TAU = 16.0
GLA_CHUNK = 32
NORM_EPS = 1e-5
DEEPNORM_ALPHA = (2.0 * DEPTH) ** 0.25
MASK_VALUE = -1e30
LOG2_E = math.log2(math.e)
HALF_PAGES = N_PAGES // 2

LANES = 128
SUBLANES = 8
VMEM_LIMIT = 56 * 1024 * 1024

F32 = jnp.float32
BF16 = jnp.bfloat16
NT_DIMS = (((1,), (1,)), ((), ()))
TN_DIMS = (((0,), (0,)), ((), ()))


def _params(*sem):
    return pltpu.CompilerParams(dimension_semantics=sem, vmem_limit_bytes=VMEM_LIMIT)


def _sigmoid(x):
    return 1.0 / (1.0 + jnp.exp(-x))


def _rope_lanes(zj, cos, sina, sinb):
    return zj * cos + pltpu.roll(zj, LANES - 32, 1) * sina + pltpu.roll(zj, 32, 1) * sinb


def _rope_rows(zt, cost, sint):
    half = DA_DK // 2
    z4 = zt.reshape(2 * DA_HEADS, 2, half, zt.shape[1])
    x1, x2 = z4[:, 0], z4[:, 1]
    ct, st = cost[None], sint[None]
    return jnp.stack([x1 * ct - x2 * st, x2 * ct + x1 * st], axis=1).reshape(zt.shape)


def _rope_rows_token_major(z, cos, sina, sinb, store):
    for j in range(D_MODEL // LANES):
        sl = slice(j * LANES, (j + 1) * LANES)
        store(sl, _rope_lanes(z[:, sl], cos, sina, sinb))


def _proj_attn_prompt_kernel(x_ref, wqt_ref, wkt_ref, wvt_ref, cos_ref, sina_ref, sinb_ref, cost_ref, sint_ref,
                             *rest):
    qt_ref, kt_ref, kb_ref, vf_ref, vtb_ref = rest[-5:]
    xb = x_ref[0].astype(BF16)
    cost, sint = cost_ref[...], sint_ref[...]
    tn = lambda w_ref: lax.dot_general(w_ref[...], xb, NT_DIMS, preferred_element_type=F32)
    nt = lambda w_ref: lax.dot_general(xb, w_ref[...], NT_DIMS, preferred_element_type=F32)
    qt_ref[0] = (_rope_rows(tn(wqt_ref), cost, sint) * (DA_DK ** -0.5 * LOG2_E)).astype(BF16)
    kt_ref[0] = _rope_rows(tn(wkt_ref), cost, sint)

    def store_k(sl, val):
        kb_ref[0, :, sl] = val.astype(BF16)
    _rope_rows_token_major(nt(wkt_ref), cos_ref[...], sina_ref[...], sinb_ref[...], store_k)
    vf_ref[0] = nt(wvt_ref)
    vtb_ref[0, 0] = tn(wvt_ref).astype(BF16)


def _proj_attn_sample_kernel(x_ref, wqt_ref, wkt_ref, wvt_ref, cos_ref, sina_ref, sinb_ref,
                             q_ref, k_ref, v_ref):
    xb = x_ref[0].astype(BF16)
    cos, sina, sinb = cos_ref[...], sina_ref[...], sinb_ref[...]
    nt = lambda w_ref: lax.dot_general(xb, w_ref[...], NT_DIMS, preferred_element_type=F32)

    def store_q(sl, val):
        q_ref[0, :, sl] = (val * (DA_DK ** -0.5)).astype(BF16)

    def store_k(sl, val):
        k_ref[0, :, sl] = val
    _rope_rows_token_major(nt(wqt_ref), cos, sina, sinb, store_q)
    _rope_rows_token_major(nt(wkt_ref), cos, sina, sinb, store_k)
    v_ref[0] = nt(wvt_ref)


def _proj_attn(x, wqt, wkt, wvt, tables, tables_t, tm, prompt, layer=0, prev_kv=None):
    b, t, _ = x.shape
    nt = tables[0].shape[0] // tm
    tab = lambda bi, i: (i % nt, 0)
    tab_t = lambda bi, i: (0, i % nt)
    wspec = pl.BlockSpec((D_MODEL, D_MODEL), lambda bi, i: (0, 0))
    tok = pl.BlockSpec((1, tm, D_MODEL), lambda bi, i: (bi, i, 0))
    tok_t = pl.BlockSpec((1, D_MODEL, tm), lambda bi, i: (bi, 0, i))
    tabs = [pl.BlockSpec((tm, LANES), tab)] * 3
    shape = lambda s, dt: jax.ShapeDtypeStruct(s, dt)
    aliases = {}
    if prompt:
        body = _proj_attn_prompt_kernel
        in_specs = [tok, wspec, wspec, wspec] + tabs + [pl.BlockSpec((DA_DK // 2, tm), tab_t)] * 2
        args = [x, wqt, wkt, wvt, *tables, *tables_t]
        if prev_kv is not None:
            aliases = {len(args): 1, len(args) + 1: 3}
            in_specs += [pl.BlockSpec(memory_space=pl.ANY)] * 2
            args += list(prev_kv)
        out_specs = [tok_t,
                     pl.BlockSpec((None, 1, D_MODEL, tm), lambda bi, i: (layer, bi, 0, i)),
                     tok,
                     pl.BlockSpec((None, 1, tm, D_MODEL), lambda bi, i: (layer, bi, i, 0)),
                     pl.BlockSpec((1, 1, D_MODEL, tm), lambda bi, i: (bi, i, 0, 0))]
        out_shape = [shape((b, D_MODEL, t), BF16), shape((DEPTH, b, D_MODEL, t), F32),
                     shape((b, t, D_MODEL), BF16), shape((DEPTH, b, t, D_MODEL), F32),
                     shape((b, t // tm, D_MODEL, tm), BF16)]
    else:
        body = _proj_attn_sample_kernel
        in_specs = [tok, wspec, wspec, wspec] + tabs
        args = [x, wqt, wkt, wvt, *tables]
        out_specs = [tok, tok, tok]
        out_shape = [shape((b, t, D_MODEL), BF16), shape((b, t, D_MODEL), F32), shape((b, t, D_MODEL), F32)]
    return pl.pallas_call(
        body, grid=(b, t // tm), in_specs=in_specs, out_specs=out_specs, out_shape=out_shape,
        input_output_aliases=aliases,
        compiler_params=_params("parallel", "parallel"),
        name="proj_attn_prompt" if prompt else "proj_attn_sample",
    )(*args)


def _proj_gla_kernel(x_ref, w_ref, wlr_ref, wa2_ref, ba_ref,
                     gq_ref, gk_ref, gv_ref, la_ref):
    xb = x_ref[...].astype(BF16)
    z = jnp.dot(xb, w_ref[...], preferred_element_type=F32)
    gq_ref[...] = z[:, :GLA_KEY]
    gk_ref[...] = z[:, GLA_KEY:2 * GLA_KEY]
    gv_ref[...] = z[:, 2 * GLA_KEY:]
    g_lr = jnp.dot(xb, wlr_ref[...], preferred_element_type=F32)
    logit = jnp.dot(g_lr.astype(BF16), wa2_ref[...], preferred_element_type=F32) + ba_ref[...]
    log_sig = jnp.minimum(logit, 0.0) - jnp.log1p(jnp.exp(-jnp.abs(logit)))
    la_ref[...] = log_sig / GLA_TAU


def _proj_gla(x, w_g, w_lr, w_a2, b_a, tm):
    n = x.shape[0]
    row = lambda i: (i, 0)
    const = lambda i: (0, 0)
    return pl.pallas_call(
        _proj_gla_kernel,
        grid=(n // tm,),
        in_specs=[pl.BlockSpec((tm, D_MODEL), row),
                  pl.BlockSpec((D_MODEL, 2 * D_MODEL), const),
                  pl.BlockSpec((D_MODEL, LANES), const),
                  pl.BlockSpec((LANES, GLA_KEY), const),
                  pl.BlockSpec((1, GLA_KEY), const)],
        out_specs=[pl.BlockSpec((tm, GLA_KEY), row), pl.BlockSpec((tm, GLA_KEY), row),
                   pl.BlockSpec((tm, D_MODEL), row), pl.BlockSpec((tm, GLA_KEY), row)],
        out_shape=[jax.ShapeDtypeStruct((n, GLA_KEY), F32),
                   jax.ShapeDtypeStruct((n, GLA_KEY), F32),
                   jax.ShapeDtypeStruct((n, D_MODEL), F32),
                   jax.ShapeDtypeStruct((n, GLA_KEY), F32)],
        compiler_params=_params("parallel"),
        name="proj_gla",
    )(x, w_g, w_lr, w_a2, b_a)


def _proj_gates_kernel(x_ref, w_ref, ga_ref, gb_ref):
    xb = x_ref[...].astype(BF16)
    z = jnp.dot(xb, w_ref[...], preferred_element_type=F32)
    dg, ma = z[:, :D_MODEL], z[:, D_MODEL:2 * D_MODEL]
    gg, mb = z[:, 2 * D_MODEL:3 * D_MODEL], z[:, 3 * D_MODEL:]
    ga_ref[...] = _sigmoid(ma) * (dg * _sigmoid(dg))
    gb_ref[...] = _sigmoid(mb) * (gg * _sigmoid(gg))


def _proj_gates(x, w_gates, tm):
    n = x.shape[0]
    row = lambda i: (i, 0)
    return pl.pallas_call(
        _proj_gates_kernel,
        grid=(n // tm,),
        in_specs=[pl.BlockSpec((tm, D_MODEL), row),
                  pl.BlockSpec((D_MODEL, 4 * D_MODEL), lambda i: (0, 0))],
        out_specs=[pl.BlockSpec((tm, D_MODEL), row)] * 2,
        out_shape=[jax.ShapeDtypeStruct((n, D_MODEL), F32)] * 2,
        compiler_params=_params("parallel"),
        name="proj_gates",
    )(x, w_gates)


def _lambda(lam_ref, lam_init):
    lp = lam_ref[...]
    a = jnp.sum(lp[0:1] * lp[1:2], axis=1, keepdims=True)
    b = jnp.sum(lp[2:3] * lp[3:4], axis=1, keepdims=True)
    return jnp.exp(a) - jnp.exp(b) + lam_init


def _head_norm(o, w, gain):
    return o * lax.rsqrt(jnp.mean(o * o, axis=-1, keepdims=True) + NORM_EPS) * w * gain


def _prompt_attention(i, lam, qt_ref, k_ref, vt_ref, ga_ref, nw_ref, o_ref,
                      qq_scr, sa_scr, sb_scr, m_scr, l_scr, acc_scr, *, tq, tk, lam_init):
    qt = qt_ref[0]
    row = lax.broadcasted_iota(jnp.int32, (LANES, tq), 0)
    zero = jnp.zeros_like(qt)
    qq_scr[:, :tq] = jnp.where(row < DA_DK, qt, zero)
    qq_scr[:, tq:] = jnp.where(row >= DA_DK, qt, zero)
    m_scr[...] = jnp.full(m_scr.shape, -jnp.inf, F32)
    l_scr[...] = jnp.zeros(l_scr.shape, F32)
    acc_scr[...] = jnp.zeros(acc_scr.shape, F32)

    def scores(j, s_ref):
        k = k_ref[0, pl.ds(pl.multiple_of(j * tk, tk), tk), :]
        s_ref[...] = jnp.dot(k, qq_scr[...], preferred_element_type=F32)

    def softmax_pv(j, s_ref, masked):
        s = s_ref[...]
        if masked:
            r = lax.broadcasted_iota(jnp.int32, (tk, 2 * tq), 0)
            c = lax.broadcasted_iota(jnp.int32, (tk, 2 * tq), 1)
            qpos = i * tq + jnp.where(c >= tq, c - tq, c)
            s = jnp.where(j * tk + r <= qpos, s, MASK_VALUE)
        m_old = m_scr[...]
        m_new = jnp.maximum(m_old, jnp.max(s, axis=0, keepdims=True))
        alpha = jnp.exp2(m_old - m_new)
        p = jnp.exp2(s - m_new)
        l_scr[...] = alpha * l_scr[...] + jnp.sum(p, axis=0, keepdims=True)
        acc_scr[...] = alpha * acc_scr[...] + jnp.dot(vt_ref[0, j], p.astype(BF16),
                                                      preferred_element_type=F32)
        m_scr[...] = m_new

    assert tq == tk
    scores(0, sa_scr)

    def body(t, carry):
        scores(2 * t + 1, sb_scr)
        softmax_pv(2 * t, sa_scr, False)
        scores(2 * t + 2, sa_scr)
        softmax_pv(2 * t + 1, sb_scr, False)
        return carry
    lax.fori_loop(0, i // 2, body, 0)

    @pl.when(i % 2 == 1)
    def _():
        scores(i, sb_scr)
        softmax_pv(i - 1, sa_scr, False)
        softmax_pv(i, sb_scr, True)

    @pl.when(i % 2 == 0)
    def _():
        softmax_pv(i, sa_scr, True)

    ot = acc_scr[...] / l_scr[...]
    o = (ot[:, :tq] - lam * ot[:, tq:]).T
    o_ref[0] = _head_norm(o, nw_ref[...], 1.0 - lam_init) * ga_ref[0]


def _sample_attention_half(half, lam, q_ref, kn_ref, vn_ref, ga_ref, nw_ref, k_pages, v_pages, o_ref,
                           kb_scr, vb_scr, m_scr, l_scr, acc_scr, *, t_new, lam_init):
    n_rows = 2 * DA_HEADS * t_new

    for p in range(HALF_PAGES):
        kb_scr[:, p * PAGE_SIZE:(p + 1) * PAGE_SIZE] = k_pages[p][...].astype(BF16)
        for h in range(DA_HEADS):
            vh = v_pages[p][pl.ds(h, PAGE_SIZE, stride=DA_HEADS), :]
            vb_scr[p * PAGE_SIZE:(p + 1) * PAGE_SIZE, h * DA_DV:(h + 1) * DA_DV] = vh.astype(BF16)

    qf = q_ref[0].astype(F32)
    q_rep = jnp.concatenate([qf] * (2 * DA_HEADS), axis=0)
    r = lax.broadcasted_iota(jnp.int32, (n_rows, D_MODEL), 0)
    c = lax.broadcasted_iota(jnp.int32, (n_rows, D_MODEL), 1)
    q_exp = jnp.where(c // DA_DK == r // t_new, q_rep, 0.0).astype(BF16)

    @pl.when(half == 0)
    def _():
        m_scr[...] = jnp.full(m_scr.shape, -jnp.inf, F32)
        l_scr[...] = jnp.zeros(l_scr.shape, F32)
        acc_scr[...] = jnp.zeros(acc_scr.shape, F32)

    def update(s, v):
        m_old = m_scr[...]
        m_new = jnp.maximum(m_old, jnp.max(s, axis=1, keepdims=True))
        alpha = jnp.exp(m_old - m_new)
        p = jnp.exp(s - m_new)
        l_scr[...] = alpha * l_scr[...] + jnp.sum(p, axis=1, keepdims=True)
        acc_scr[...] = alpha * acc_scr[...] + jnp.dot(p.astype(BF16), v, preferred_element_type=F32)
        m_scr[...] = m_new

    update(jnp.dot(q_exp, kb_scr[...], preferred_element_type=F32), vb_scr[...])

    @pl.when(half == 1)
    def _():
        pad = jnp.zeros((PAGE_SIZE - t_new, D_MODEL), F32)
        k_new = jnp.concatenate([kn_ref[0], pad], axis=0).astype(BF16)
        v_new = jnp.concatenate([vn_ref[0], pad], axis=0).astype(BF16)
        s_new = lax.dot_general(q_exp, k_new, NT_DIMS, preferred_element_type=F32)
        rr = lax.broadcasted_iota(jnp.int32, s_new.shape, 0)
        cc = lax.broadcasted_iota(jnp.int32, s_new.shape, 1)
        update(jnp.where(cc <= rr % t_new, s_new, MASK_VALUE), v_new)
        out = acc_scr[...] / l_scr[...]
        heads = []
        for h in range(DA_HEADS):
            cols = slice(h * DA_DV, (h + 1) * DA_DV)
            o0 = out[(2 * h) * t_new:(2 * h + 1) * t_new, cols]
            o1 = out[(2 * h + 1) * t_new:(2 * h + 2) * t_new, cols]
            heads.append(_head_norm(o0 - lam * o1, nw_ref[...], 1.0 - lam_init))
        o_ref[0] = jnp.concatenate(heads, axis=1) * ga_ref[0]


N_PROMPT_IN, N_SAMPLE_IN = 4, 4
N_PROMPT_SCR = 6


def _attn_kernel(pt_ref, nw_ref, lam_ref, *refs, tq, tk, t_new, lam_init):
    i = pl.program_id(2)
    prompt_in = refs[:N_PROMPT_IN]
    sample_in = refs[N_PROMPT_IN:N_PROMPT_IN + N_SAMPLE_IN]
    pages = refs[N_PROMPT_IN + N_SAMPLE_IN:N_PROMPT_IN + N_SAMPLE_IN + 2 * HALF_PAGES]
    o_ref, os_ref = refs[N_PROMPT_IN + N_SAMPLE_IN + 2 * HALF_PAGES:][:2]
    scratch = refs[N_PROMPT_IN + N_SAMPLE_IN + 2 * HALF_PAGES + 2:]
    lam = _lambda(lam_ref, lam_init)
    _sample_attention_half(i % 2, lam, *sample_in, nw_ref, pages[:HALF_PAGES], pages[HALF_PAGES:], os_ref,
                           *scratch[N_PROMPT_SCR:], t_new=t_new, lam_init=lam_init)
    _prompt_attention(i, lam, *prompt_in, nw_ref, o_ref, *scratch[:N_PROMPT_SCR],
                      tq=tq, tk=tk, lam_init=lam_init)


def _attention(page_table, qt, k, vt, ga, q_s, k_new, v_new, ga_s, norm_w, lam_p, cache_kt, cache_v,
               layer, lam_init, tq, tk):
    b, t, _ = k.shape
    nb, t_new, _ = q_s.shape
    nq, nk = t // tq, t // tk
    assert vt.shape == (b, nk, D_MODEL, tk)
    assert nq % 2 == 0 and b * DA_HEADS * nq == 2 * nb
    sample = lambda bi, h, i: (bi * DA_HEADS + h) * (nq // 2) + i // 2
    blk = pl.BlockSpec((1, tq, LANES), lambda bi, h, i, pt: (bi, i, h))
    tok = pl.BlockSpec((1, t_new, D_MODEL), lambda bi, h, i, pt: (sample(bi, h, i), 0, 0))

    def page_spec(p):
        return pl.BlockSpec(
            (None, None, D_MODEL, PAGE_SIZE),
            lambda bi, h, i, pt: (layer, pt[sample(bi, h, i) * N_PAGES + (i % 2) * HALF_PAGES + p], 0, 0))

    pages = [page_spec(p) for p in range(HALF_PAGES)]
    n_rows = 2 * DA_HEADS * t_new
    return pl.pallas_call(
        functools.partial(_attn_kernel, tq=tq, tk=tk, t_new=t_new, lam_init=lam_init),
        grid_spec=pltpu.PrefetchScalarGridSpec(
            num_scalar_prefetch=1,
            grid=(b, DA_HEADS, nq),
            in_specs=[pl.BlockSpec((1, LANES), lambda bi, h, i, pt: (0, 0)),
                      pl.BlockSpec((4, DA_DK), lambda bi, h, i, pt: (0, 0)),
                      pl.BlockSpec((1, LANES, tq), lambda bi, h, i, pt: (bi, h, i)),
                      pl.BlockSpec((1, t, LANES), lambda bi, h, i, pt: (bi, 0, h)),
                      pl.BlockSpec((1, nk, LANES, tk), lambda bi, h, i, pt: (bi, 0, h, 0)),
                      blk,
                      tok, tok, tok, tok] + pages + pages,
            out_specs=[blk, tok],
            scratch_shapes=[pltpu.VMEM((LANES, 2 * tq), BF16),
                            pltpu.VMEM((tk, 2 * tq), F32),
                            pltpu.VMEM((tk, 2 * tq), F32),
                            pltpu.VMEM((1, 2 * tq), F32),
                            pltpu.VMEM((1, 2 * tq), F32),
                            pltpu.VMEM((LANES, 2 * tq), F32),
                            pltpu.VMEM((D_MODEL, HALF_PAGES * PAGE_SIZE), BF16),
                            pltpu.VMEM((HALF_PAGES * PAGE_SIZE, D_MODEL), BF16),
                            pltpu.VMEM((n_rows, 1), F32),
                            pltpu.VMEM((n_rows, 1), F32),
                            pltpu.VMEM((n_rows, D_MODEL), F32)]),
        out_shape=[jax.ShapeDtypeStruct((b, t, D_MODEL), F32),
                   jax.ShapeDtypeStruct((nb, t_new, D_MODEL), F32)],
        compiler_params=_params("arbitrary", "arbitrary", "arbitrary"),
        name="attention",
    )(page_table.reshape(-1), norm_w, lam_p, qt, k, vt, ga, q_s, k_new, v_new, ga_s,
      *([cache_kt] * HALF_PAGES), *([cache_v] * HALF_PAGES))


def _split3(x):
    hi = x.astype(BF16)
    r1 = x - hi.astype(F32)
    mid = r1.astype(BF16)
    lo = (r1 - mid.astype(F32)).astype(BF16)
    return hi, mid, lo


def _gla_kernel(*refs, bb, t_blk, rows, chunk, has_s0):
    q_ref, k_ref, v_ref, la_ref, gb_ref, nw_ref = refs[:6]
    y_ref, sf_ref, s_scr = refs[-3:]
    blk = pl.program_id(1)
    n_chunks = rows // chunk
    chains = [(bi, h) for bi in range(bb) for h in range(GLA_HEADS)]

    @pl.when(blk == 0)
    def _():
        if has_s0:
            s_scr[...] = refs[6][...]
        else:
            s_scr[...] = jnp.zeros(s_scr.shape, F32)

    def load(ref, bi, lo, width):
        x = ref[bi, :, lo:lo + width]
        if t_blk < rows:
            x = jnp.concatenate([x, jnp.zeros((rows - t_blk, width), F32)], axis=0)
        return x

    r = lax.broadcasted_iota(jnp.int32, (rows, rows), 0)
    c = lax.broadcasted_iota(jnp.int32, (rows, rows), 1)
    tri = (r // chunk == c // chunk) & (c <= r)
    tri_b = tri.astype(BF16)

    q_dec, k_end, vb, o_intra, decay_t, s = [], [], [], [], [], []
    for bi, h in chains:
        q, k, la = (load(ref, bi, h * GLA_DK, GLA_DK) for ref in (q_ref, k_ref, la_ref))
        v = load(v_ref, bi, h * GLA_DV, GLA_DV)
        b = sum(jnp.dot(tri_b, part, preferred_element_type=F32) for part in _split3(la))
        qd = (q * jnp.exp(b) * (GLA_DK ** -0.5)).astype(BF16)
        k_dec = (k * jnp.exp(-b)).astype(BF16)
        a = lax.dot_general(qd, k_dec, NT_DIMS, preferred_element_type=F32)
        a = jnp.where(tri, a, 0.0)
        vb.append(v.astype(BF16))
        q_dec.append(qd)
        o_intra.append(jnp.dot(a.astype(BF16), vb[-1], preferred_element_type=F32))
        b3 = b.reshape(n_chunks, chunk, GLA_DK)
        b_last = b3[:, chunk - 1:chunk, :]
        k_end.append((k.reshape(n_chunks, chunk, GLA_DK) * jnp.exp(b_last - b3))
                     .reshape(rows, GLA_DK).astype(BF16))
        decay = jnp.exp(b_last.reshape(n_chunks, GLA_DK))
        if n_chunks == 1:
            decay = jnp.broadcast_to(decay, (SUBLANES, GLA_DK))
        decay_t.append(decay.T)
        s.append(s_scr[bi, h])

    outs = [[] for _ in chains]
    for ci in range(n_chunks):
        sl = slice(ci * chunk, (ci + 1) * chunk)
        for n in range(len(chains)):
            outs[n].append(jnp.dot(q_dec[n][sl], s[n].astype(BF16), preferred_element_type=F32))
            ds = lax.dot_general(k_end[n][sl], vb[n][sl], TN_DIMS, preferred_element_type=F32)
            s[n] = decay_t[n][:, ci:ci + 1] * s[n] + ds

    for n, (bi, h) in enumerate(chains):
        s_scr[bi, h] = s[n]
        o = o_intra[n] + (jnp.concatenate(outs[n], axis=0) if n_chunks > 1 else outs[n][0])
        y = o * lax.rsqrt(jnp.mean(o * o, axis=-1, keepdims=True) + NORM_EPS) * nw_ref[...]
        cols = slice(h * GLA_DV, (h + 1) * GLA_DV)
        y_ref[bi, :, cols] = y[:t_blk] * gb_ref[bi, :, cols]

    @pl.when(blk == pl.num_programs(1) - 1)
    def _():
        for n, (bi, h) in enumerate(chains):
            sf_ref[bi, h] = s[n]


def _gla(gq, gk, gv, la, gb, norm_w, state, layer, prev_states, t_blk, bb):
    b, t, _ = gq.shape
    rows = max(t_blk, GLA_CHUNK)
    assert rows // GLA_CHUNK in (1, SUBLANES) and b % bb == 0
    kspec = pl.BlockSpec((bb, t_blk, GLA_KEY), lambda bi, i: (bi, i, 0))
    vspec = pl.BlockSpec((bb, t_blk, D_MODEL), lambda bi, i: (bi, i, 0))
    sspec = pl.BlockSpec((None, bb, GLA_HEADS, GLA_DK, GLA_DV), lambda bi, i: (layer, bi, 0, 0, 0))
    in_specs = [kspec, kspec, vspec, kspec, vspec, pl.BlockSpec((1, GLA_DV), lambda bi, i: (0, 0))]
    args = [gq, gk, gv, la, gb, norm_w]
    if state is not None:
        in_specs.append(sspec)
        args.append(state)
    aliases = {}
    if prev_states is not None:
        aliases = {len(args): 1}
        in_specs.append(pl.BlockSpec(memory_space=pl.ANY))
        args.append(prev_states)
    return pl.pallas_call(
        functools.partial(_gla_kernel, bb=bb, t_blk=t_blk, rows=rows, chunk=GLA_CHUNK,
                          has_s0=state is not None),
        grid=(b // bb, t // t_blk),
        in_specs=in_specs,
        out_specs=[vspec, sspec],
        out_shape=[jax.ShapeDtypeStruct((b, t, D_MODEL), F32),
                   jax.ShapeDtypeStruct((DEPTH, b, GLA_HEADS, GLA_DK, GLA_DV), F32)],
        scratch_shapes=[pltpu.VMEM((bb, GLA_HEADS, GLA_DK, GLA_DV), F32)],
        input_output_aliases=aliases,
        compiler_params=_params("parallel", "arbitrary"),
        name="gla",
    )(*args)


def _out_proj_kernel(ya_ref, yb_ref, x_ref, w_ref, lw_ref, lb_ref, o_ref):
    merged = (ya_ref[...] + yb_ref[...]).astype(BF16)
    h = jnp.dot(merged, w_ref[...], preferred_element_type=F32)
    y = DEEPNORM_ALPHA * x_ref[...] + h
    mu = jnp.mean(y, axis=-1, keepdims=True)
    d = y - mu
    var = jnp.mean(d * d, axis=-1, keepdims=True)
    o_ref[...] = d * lax.rsqrt(var + NORM_EPS) * lw_ref[...] + lb_ref[...]


def _out_proj(ya, yb, x, w_out, ln_w, ln_b, tm):
    n = x.shape[0]
    row = lambda i: (i, 0)
    const = lambda i: (0, 0)
    return pl.pallas_call(
        _out_proj_kernel,
        grid=(n // tm,),
        in_specs=[pl.BlockSpec((tm, D_MODEL), row)] * 3
                 + [pl.BlockSpec((D_MODEL, D_MODEL), const),
                    pl.BlockSpec((1, D_MODEL), const), pl.BlockSpec((1, D_MODEL), const)],
        out_specs=pl.BlockSpec((tm, D_MODEL), row),
        out_shape=jax.ShapeDtypeStruct((n, D_MODEL), F32),
        compiler_params=_params("parallel"),
        name="out_proj",
    )(ya, yb, x, w_out, ln_w, ln_b)


def _rope_tables(pos):
    half = DA_DK // 2
    inv = ROPE_THETA ** (-jnp.arange(0, DA_DK, 2, dtype=F32) / DA_DK)
    ang = pos.astype(F32)[:, None] * inv[None, :]
    cos32, sin32 = jnp.cos(ang), jnp.sin(ang)
    cos = jnp.tile(cos32, (1, LANES // half))
    sin = jnp.tile(sin32, (1, LANES // half))
    first = (jnp.arange(LANES) % DA_DK) < half
    return (cos, jnp.where(first, -sin, 0.0), jnp.where(first, 0.0, sin)), (cos32.T, sin32.T)


def _layer_weights(layer, w_in, gla_w_a2, gla_b_a, w_out):
    w = w_in[layer]
    o = [0]
    for s in (1024, 1024, 1024, 1024, 512, 512, 1024, 1024, 16, 1024, 1024):
        o.append(o[-1] + s)
    seg = lambda i: w[:, o[i]:o[i + 1]]
    return dict(
        wqt=seg(0).T.astype(BF16), wkt=seg(1).T.astype(BF16), wvt=seg(2).T.astype(BF16),
        w_g=jnp.concatenate([seg(4), seg(5), seg(6)], axis=1).astype(BF16),
        w_gates=jnp.concatenate([seg(3), seg(9), seg(7), seg(10)], axis=1).astype(BF16),
        w_lr=jnp.pad(seg(8), ((0, 0), (0, LANES - GLA_RANK))).astype(BF16),
        w_a2=jnp.pad(gla_w_a2[layer], ((0, LANES - GLA_RANK), (0, 0))).astype(BF16),
        b_a=gla_b_a[layer][None, :],
        w_out=w_out[layer].astype(BF16))


def kernel(x_prompt, x_sample, cache_k, cache_v, state_gla, page_table, w_in,
           lam_q1, lam_k1, lam_q2, lam_k2, da_norm_w, gla_w_a2, gla_b_a, gla_norm_w,
           w_out, ln_w, ln_b):
    bp, tp, _ = x_prompt.shape
    bs, ts, _ = x_sample.shape
    ns = bs * ts
    tm = 512
    tm_s = min(tm, ns)
    tab_p, tabt_p = _rope_tables(jnp.arange(tp))
    tab_s, _ = _rope_tables(PAST_LEN + jnp.arange(ts))
    tab_s = tuple(jnp.tile(t, (tm_s // ts, 1)) for t in tab_s)
    xp = x_prompt.reshape(bp * tp, D_MODEL)
    xs = x_sample.reshape(ns, D_MODEL)
    n_pool = cache_k.shape[1]
    cache_kt = jnp.transpose(cache_k, (0, 1, 3, 4, 5, 2)).reshape(DEPTH, n_pool, D_MODEL, PAGE_SIZE)
    cache_v2 = cache_v.reshape(DEPTH, n_pool, PAGE_SIZE * DA_HEADS, DA_DV)

    ks_rows, vs_rows = [], []
    kv_p = sp_all = ss_all = None
    for layer in range(DEPTH):
        lam_init = 0.8 - 0.6 * math.exp(-0.3 * layer)
        w = _layer_weights(layer, w_in, gla_w_a2, gla_b_a, w_out)
        lam_p = jnp.stack([lam_q1[layer], lam_k1[layer], lam_q2[layer], lam_k2[layer]])
        da_w = da_norm_w[layer][None, :]
        gla_w = gla_norm_w[layer][None, :]
        lw, lb = ln_w[layer][None, :], ln_b[layer][None, :]
        sh3 = lambda a, b: a.reshape(b, -1, a.shape[-1])

        qt, kt_all, kb, vf_all, vtb = _proj_attn(sh3(xp, bp), w["wqt"], w["wkt"], w["wvt"], tab_p, tabt_p, tm,
                                                 True, layer, kv_p)
        kv_p = (kt_all, vf_all)
        gq_p, gk_p, gv_p, la_p = _proj_gla(xp, w["w_g"], w["w_lr"], w["w_a2"], w["b_a"], tm)
        ga_p, gb_p = _proj_gates(xp, w["w_gates"], tm)
        q, kf, vf = _proj_attn(sh3(xs, 1), w["wqt"], w["wkt"], w["wvt"], tab_s, None, tm_s, False)
        gq_s, gk_s, gv_s, la_s = _proj_gla(xs, w["w_g"], w["w_lr"], w["w_a2"], w["b_a"], tm_s)
        ga_s, gb_s = _proj_gates(xs, w["w_gates"], tm_s)
        ya_p, ya_s = _attention(page_table, qt, kb, vtb, sh3(ga_p, bp),
                                sh3(q, bs), sh3(kf, bs), sh3(vf, bs), sh3(ga_s, bs), da_w, lam_p,
                                cache_kt, cache_v2, layer, lam_init, tq=tm, tk=tm)

        yb, sp_all = _gla(sh3(gq_p, bp), sh3(gk_p, bp), sh3(gv_p, bp), sh3(la_p, bp), sh3(gb_p, bp), gla_w,
                          None, layer, sp_all, 256, bp)
        xp = _out_proj(ya_p.reshape(-1, D_MODEL), yb.reshape(-1, D_MODEL), xp, w["w_out"], lw, lb, tm)
        yb, ss_all = _gla(sh3(gq_s, bs), sh3(gk_s, bs), sh3(gv_s, bs), sh3(la_s, bs), sh3(gb_s, bs), gla_w,
                          state_gla, layer, ss_all, ts, 4)
        xs = _out_proj(ya_s.reshape(-1, D_MODEL), yb.reshape(-1, D_MODEL), xs, w["w_out"], lw, lb, tm_s)
        ks_rows.append(kf.reshape(bs, ts, DA_HEADS, 2, DA_DK))
        vs_rows.append(vf.reshape(bs, ts, DA_HEADS, DA_DV))

    kt_all, vf_all = kv_p
    new_k_p = jnp.transpose(kt_all.reshape(DEPTH, bp, DA_HEADS, 2, DA_DK, tp), (0, 1, 5, 2, 3, 4))
    return (xp.reshape(bp, tp, D_MODEL), xs.reshape(bs, ts, D_MODEL),
            new_k_p, vf_all.reshape(DEPTH, bp, tp, DA_HEADS, DA_DV), sp_all,
            jnp.stack(ks_rows), jnp.stack(vs_rows), ss_all)
```

```python
import functools
import math

import jax
import jax.numpy as jnp
from jax import lax
from jax.experimental import pallas as pl
from jax.experimental.pallas import tpu as pltpu

D_MODEL = 1024
DEPTH = 2
PAST_LEN = 2048
PAGE_SIZE = 128
N_PAGES = PAST_LEN // PAGE_SIZE
DA_HEADS = 8
DA_DK = 64
DA_DV = 128
ROPE_THETA = 10000.0
GLA_HEADS = 4
GLA_KEY = 512
GLA_DK = 128
GLA_DV = 256
GLA_RANK = 16
GLA_TAU = 16.0
GLA_CHUNK = 32
NORM_EPS = 1e-5
DEEPNORM_ALPHA = (2.0 * DEPTH) ** 0.25
MASK_VALUE = -1e30
LOG2_E = math.log2(math.e)

LANES = 128
SUBLANES = 8
VMEM_LIMIT = 56 * 1024 * 1024

F32 = jnp.float32
BF16 = jnp.bfloat16
NT_DIMS = (((1,), (1,)), ((), ()))
TN_DIMS = (((0,), (0,)), ((), ()))


def _params(*sem):
    return pltpu.CompilerParams(dimension_semantics=sem, vmem_limit_bytes=VMEM_LIMIT)


def _sigmoid(x):
    return 1.0 / (1.0 + jnp.exp(-x))


def _rope_lanes(zj, cos, sina, sinb):
    return zj * cos + pltpu.roll(zj, LANES - 32, 1) * sina + pltpu.roll(zj, 32, 1) * sinb


def _rope_rows(zt, cost, sint):
    half = DA_DK // 2
    z4 = zt.reshape(2 * DA_HEADS, 2, half, zt.shape[1])
    x1, x2 = z4[:, 0], z4[:, 1]
    ct, st = cost[None], sint[None]
    return jnp.stack([x1 * ct - x2 * st, x2 * ct + x1 * st], axis=1).reshape(zt.shape)


def _rope_rows_token_major(z, cos, sina, sinb, store):
    for j in range(D_MODEL // LANES):
        sl = slice(j * LANES, (j + 1) * LANES)
        store(sl, _rope_lanes(z[:, sl], cos, sina, sinb))


def _proj_attn_prompt_kernel(x_ref, wqt_ref, wkt_ref, wvt_ref, cos_ref, sina_ref, sinb_ref, cost_ref, sint_ref,
                             *rest):
    qt_ref, kt_ref, kb_ref, vf_ref, vtb_ref = rest[-5:]
    xb = x_ref[0].astype(BF16)
    cost, sint = cost_ref[...], sint_ref[...]
    tn = lambda w_ref: lax.dot_general(w_ref[...], xb, NT_DIMS, preferred_element_type=F32)
    nt = lambda w_ref: lax.dot_general(xb, w_ref[...], NT_DIMS, preferred_element_type=F32)
    qt_ref[0] = (_rope_rows(tn(wqt_ref), cost, sint) * (DA_DK ** -0.5 * LOG2_E)).astype(BF16)
    kt_ref[0] = _rope_rows(tn(wkt_ref), cost, sint)

    def store_k(sl, val):
        kb_ref[0, :, sl] = val.astype(BF16)
    _rope_rows_token_major(nt(wkt_ref), cos_ref[...], sina_ref[...], sinb_ref[...], store_k)
    vf_ref[0] = nt(wvt_ref)
    vtb_ref[0, 0] = tn(wvt_ref).astype(BF16)


def _proj_attn_sample_kernel(x_ref, wqt_ref, wkt_ref, wvt_ref, cos_ref, sina_ref, sinb_ref,
                             q_ref, k_ref, v_ref):
    xb = x_ref[0].astype(BF16)
    cos, sina, sinb = cos_ref[...], sina_ref[...], sinb_ref[...]
    nt = lambda w_ref: lax.dot_general(xb, w_ref[...], NT_DIMS, preferred_element_type=F32)

    def store_q(sl, val):
        q_ref[0, :, sl] = (val * (DA_DK ** -0.5)).astype(BF16)

    def store_k(sl, val):
        k_ref[0, :, sl] = val
    _rope_rows_token_major(nt(wqt_ref), cos, sina, sinb, store_q)
    _rope_rows_token_major(nt(wkt_ref), cos, sina, sinb, store_k)
    v_ref[0] = nt(wvt_ref)


def _proj_attn(x, wqt, wkt, wvt, tables, tables_t, tm, prompt, layer=0, prev_kv=None):
    b, t, _ = x.shape
    nt = tables[0].shape[0] // tm
    tab = lambda bi, i: (i % nt, 0)
    tab_t = lambda bi, i: (0, i % nt)
    wspec = pl.BlockSpec((D_MODEL, D_MODEL), lambda bi, i: (0, 0))
    tok = pl.BlockSpec((1, tm, D_MODEL), lambda bi, i: (bi, i, 0))
    tok_t = pl.BlockSpec((1, D_MODEL, tm), lambda bi, i: (bi, 0, i))
    tabs = [pl.BlockSpec((tm, LANES), tab)] * 3
    shape = lambda s, dt: jax.ShapeDtypeStruct(s, dt)
    aliases = {}
    if prompt:
        body = _proj_attn_prompt_kernel
        in_specs = [tok, wspec, wspec, wspec] + tabs + [pl.BlockSpec((DA_DK // 2, tm), tab_t)] * 2
        args = [x, wqt, wkt, wvt, *tables, *tables_t]
        if prev_kv is not None:
            aliases = {len(args): 1, len(args) + 1: 3}
            in_specs += [pl.BlockSpec(memory_space=pl.ANY)] * 2
            args += list(prev_kv)
        out_specs = [tok_t,
                     pl.BlockSpec((None, 1, D_MODEL, tm), lambda bi, i: (layer, bi, 0, i)),
                     tok,
                     pl.BlockSpec((None, 1, tm, D_MODEL), lambda bi, i: (layer, bi, i, 0)),
                     pl.BlockSpec((1, 1, D_MODEL, tm), lambda bi, i: (bi, i, 0, 0))]
        out_shape = [shape((b, D_MODEL, t), BF16), shape((DEPTH, b, D_MODEL, t), F32),
                     shape((b, t, D_MODEL), BF16), shape((DEPTH, b, t, D_MODEL), F32),
                     shape((b, t // tm, D_MODEL, tm), BF16)]
    else:
        body = _proj_attn_sample_kernel
        in_specs = [tok, wspec, wspec, wspec] + tabs
        args = [x, wqt, wkt, wvt, *tables]
        out_specs = [tok, tok, tok]
        out_shape = [shape((b, t, D_MODEL), BF16), shape((b, t, D_MODEL), F32), shape((b, t, D_MODEL), F32)]
    return pl.pallas_call(
        body, grid=(b, t // tm), in_specs=in_specs, out_specs=out_specs, out_shape=out_shape,
        input_output_aliases=aliases,
        compiler_params=_params("parallel", "parallel"),
        name="proj_attn_prompt" if prompt else "proj_attn_sample",
    )(*args)


def _proj_gla_kernel(x_ref, w_ref, wlr_ref, wa2_ref, ba_ref,
                     gq_ref, gk_ref, gv_ref, la_ref):
    xb = x_ref[...].astype(BF16)
    z = jnp.dot(xb, w_ref[...], preferred_element_type=F32)
    gq_ref[...] = z[:, :GLA_KEY]
    gk_ref[...] = z[:, GLA_KEY:2 * GLA_KEY]
    gv_ref[...] = z[:, 2 * GLA_KEY:]
    g_lr = jnp.dot(xb, wlr_ref[...], preferred_element_type=F32)
    logit = jnp.dot(g_lr.astype(BF16), wa2_ref[...], preferred_element_type=F32) + ba_ref[...]
    log_sig = jnp.minimum(logit, 0.0) - jnp.log1p(jnp.exp(-jnp.abs(logit)))
    la_ref[...] = log_sig / GLA_TAU


def _proj_gla(x, w_g, w_lr, w_a2, b_a, tm):
    n = x.shape[0]
    row = lambda i: (i, 0)
    const = lambda i: (0, 0)
    return pl.pallas_call(
        _proj_gla_kernel,
        grid=(n // tm,),
        in_specs=[pl.BlockSpec((tm, D_MODEL), row),
                  pl.BlockSpec((D_MODEL, 2 * D_MODEL), const),
                  pl.BlockSpec((D_MODEL, LANES), const),
                  pl.BlockSpec((LANES, GLA_KEY), const),
                  pl.BlockSpec((1, GLA_KEY), const)],
        out_specs=[pl.BlockSpec((tm, GLA_KEY), row), pl.BlockSpec((tm, GLA_KEY), row),
                   pl.BlockSpec((tm, D_MODEL), row), pl.BlockSpec((tm, GLA_KEY), row)],
        out_shape=[jax.ShapeDtypeStruct((n, GLA_KEY), F32),
                   jax.ShapeDtypeStruct((n, GLA_KEY), F32),
                   jax.ShapeDtypeStruct((n, D_MODEL), F32),
                   jax.ShapeDtypeStruct((n, GLA_KEY), F32)],
        compiler_params=_params("parallel"),
        name="proj_gla",
    )(x, w_g, w_lr, w_a2, b_a)


def _proj_gates_kernel(x_ref, w_ref, ga_ref, gb_ref):
    xb = x_ref[...].astype(BF16)
    z = jnp.dot(xb, w_ref[...], preferred_element_type=F32)
    dg, ma = z[:, :D_MODEL], z[:, D_MODEL:2 * D_MODEL]
    gg, mb = z[:, 2 * D_MODEL:3 * D_MODEL], z[:, 3 * D_MODEL:]
    ga_ref[...] = _sigmoid(ma) * (dg * _sigmoid(dg))
    gb_ref[...] = _sigmoid(mb) * (gg * _sigmoid(gg))


def _proj_gates(x, w_gates, tm):
    n = x.shape[0]
    row = lambda i: (i, 0)
    return pl.pallas_call(
        _proj_gates_kernel,
        grid=(n // tm,),
        in_specs=[pl.BlockSpec((tm, D_MODEL), row),
                  pl.BlockSpec((D_MODEL, 4 * D_MODEL), lambda i: (0, 0))],
        out_specs=[pl.BlockSpec((tm, D_MODEL), row)] * 2,
        out_shape=[jax.ShapeDtypeStruct((n, D_MODEL), F32)] * 2,
        compiler_params=_params("parallel"),
        name="proj_gates",
    )(x, w_gates)


def _lambda(lam_ref, lam_init):
    lp = lam_ref[...]
    a = jnp.sum(lp[0:1] * lp[1:2], axis=1, keepdims=True)
    b = jnp.sum(lp[2:3] * lp[3:4], axis=1, keepdims=True)
    return jnp.exp(a) - jnp.exp(b) + lam_init


def _head_norm(o, w, gain):
    return o * lax.rsqrt(jnp.mean(o * o, axis=-1, keepdims=True) + NORM_EPS) * w * gain


def _attn_prompt_kernel(qt_ref, k_ref, vt_ref, ga_ref, nw_ref, lam_ref, o_ref,
                        qq_scr, sa_scr, sb_scr, m_scr, l_scr, acc_scr, *, hg, tq, tk, lam_init):
    i = pl.program_id(2)
    row = lax.broadcasted_iota(jnp.int32, (LANES, tq), 0)
    for g in range(hg):
        qt = qt_ref[0, g * LANES:(g + 1) * LANES, :]
        zero = jnp.zeros_like(qt)
        qq_scr[g, :, :tq] = jnp.where(row < DA_DK, qt, zero)
        qq_scr[g, :, tq:] = jnp.where(row >= DA_DK, qt, zero)
    m_scr[...] = jnp.full(m_scr.shape, -jnp.inf, F32)
    l_scr[...] = jnp.zeros(l_scr.shape, F32)
    acc_scr[...] = jnp.zeros(acc_scr.shape, F32)

    def scores(j, s_ref):
        for g in range(hg):
            k = k_ref[0, pl.ds(pl.multiple_of(j * tk, tk), tk), g * LANES:(g + 1) * LANES]
            s_ref[g] = jnp.dot(k, qq_scr[g], preferred_element_type=F32)

    def softmax_pv(j, s_ref, masked):
        for g in range(hg):
            s = s_ref[g]
            if masked:
                r = lax.broadcasted_iota(jnp.int32, (tk, 2 * tq), 0)
                c = lax.broadcasted_iota(jnp.int32, (tk, 2 * tq), 1)
                qpos = i * tq + jnp.where(c >= tq, c - tq, c)
                s = jnp.where(j * tk + r <= qpos, s, MASK_VALUE)
            m_old = m_scr[g]
            m_new = jnp.maximum(m_old, jnp.max(s, axis=0, keepdims=True))
            alpha = jnp.exp2(m_old - m_new)
            p = jnp.exp2(s - m_new)
            l_scr[g] = alpha * l_scr[g] + jnp.sum(p, axis=0, keepdims=True)
            vt = vt_ref[0, j, g * LANES:(g + 1) * LANES, :]
            acc_scr[g] = alpha * acc_scr[g] + jnp.dot(vt, p.astype(BF16), preferred_element_type=F32)
            m_scr[g] = m_new

    assert tq == tk
    scores(0, sa_scr)

    def body(t, carry):
        scores(2 * t + 1, sb_scr)
        softmax_pv(2 * t, sa_scr, False)
        scores(2 * t + 2, sa_scr)
        softmax_pv(2 * t + 1, sb_scr, False)
        return carry
    lax.fori_loop(0, i // 2, body, 0)

    @pl.when(i % 2 == 1)
    def _():
        scores(i, sb_scr)
        softmax_pv(i - 1, sa_scr, False)
        softmax_pv(i, sb_scr, True)

    @pl.when(i % 2 == 0)
    def _():
        softmax_pv(i, sa_scr, True)

    lam = _lambda(lam_ref, lam_init)
    for g in range(hg):
        ot = acc_scr[g] / l_scr[g]
        o = (ot[:, :tq] - lam * ot[:, tq:]).T
        cols = slice(g * LANES, (g + 1) * LANES)
        o_ref[0, :, cols] = _head_norm(o, nw_ref[...], 1.0 - lam_init) * ga_ref[0, :, cols]


def _attn_prompt(qt, k, vt, ga, norm_w, lam_p, lam_init, tq, tk, hg):
    b, t, _ = k.shape
    nk = t // tk
    assert vt.shape == (b, nk, D_MODEL, tk) and DA_HEADS % hg == 0
    w = hg * LANES
    blk = pl.BlockSpec((1, tq, w), lambda bi, h, i: (bi, i, h))
    return pl.pallas_call(
        functools.partial(_attn_prompt_kernel, hg=hg, tq=tq, tk=tk, lam_init=lam_init),
        grid=(b, DA_HEADS // hg, t // tq),
        in_specs=[pl.BlockSpec((1, w, tq), lambda bi, h, i: (bi, h, i)),
                  pl.BlockSpec((1, t, w), lambda bi, h, i: (bi, 0, h)),
                  pl.BlockSpec((1, nk, w, tk), lambda bi, h, i: (bi, 0, h, 0)),
                  blk,
                  pl.BlockSpec((1, LANES), lambda bi, h, i: (0, 0)),
                  pl.BlockSpec((4, DA_DK), lambda bi, h, i: (0, 0))],
        out_specs=blk,
        out_shape=jax.ShapeDtypeStruct((b, t, D_MODEL), F32),
        scratch_shapes=[pltpu.VMEM((hg, LANES, 2 * tq), BF16),
                        pltpu.VMEM((hg, tk, 2 * tq), F32),
                        pltpu.VMEM((hg, tk, 2 * tq), F32),
                        pltpu.VMEM((hg, 1, 2 * tq), F32),
                        pltpu.VMEM((hg, 1, 2 * tq), F32),
                        pltpu.VMEM((hg, LANES, 2 * tq), F32)],
        compiler_params=_params("parallel", "parallel", "arbitrary"),
        name="attn_prompt",
    )(qt, k, vt, ga, norm_w, lam_p)


def _attn_sample_kernel(pt_ref, q_ref, kn_ref, vn_ref, ga_ref, nw_ref, lam_ref, *rest,
                        lam_init, t_new):
    k_pages = rest[:N_PAGES]
    v_pages = rest[N_PAGES:2 * N_PAGES]
    o_ref, kb_scr, vb_scr = rest[2 * N_PAGES:]
    n_rows = 2 * DA_HEADS * t_new
    past = N_PAGES * PAGE_SIZE

    for p in range(N_PAGES):
        kb_scr[:, p * PAGE_SIZE:(p + 1) * PAGE_SIZE] = k_pages[p][...].astype(BF16)
        for h in range(DA_HEADS):
            vh = v_pages[p][pl.ds(h, PAGE_SIZE, stride=DA_HEADS), :]
            vb_scr[p * PAGE_SIZE:(p + 1) * PAGE_SIZE, h * DA_DV:(h + 1) * DA_DV] = vh.astype(BF16)
    pad = jnp.zeros((PAGE_SIZE - t_new, D_MODEL), F32)
    k_new = jnp.concatenate([kn_ref[0], pad], axis=0).astype(BF16)
    vb_scr[past:, :] = jnp.concatenate([vn_ref[0], pad], axis=0).astype(BF16)

    qf = q_ref[0].astype(F32)
    q_rep = jnp.concatenate([qf] * (2 * DA_HEADS), axis=0)
    r = lax.broadcasted_iota(jnp.int32, (n_rows, D_MODEL), 0)
    c = lax.broadcasted_iota(jnp.int32, (n_rows, D_MODEL), 1)
    q_exp = jnp.where(c // DA_DK == r // t_new, q_rep, 0.0).astype(BF16)

    s_past = jnp.dot(q_exp, kb_scr[...], preferred_element_type=F32)
    s_new = lax.dot_general(q_exp, k_new, NT_DIMS, preferred_element_type=F32)
    rr = lax.broadcasted_iota(jnp.int32, s_new.shape, 0)
    cc = lax.broadcasted_iota(jnp.int32, s_new.shape, 1)
    s_new = jnp.where(cc <= rr % t_new, s_new, MASK_VALUE)
    m = jnp.maximum(jnp.max(s_past, axis=1, keepdims=True), jnp.max(s_new, axis=1, keepdims=True))
    p_past = jnp.exp(s_past - m)
    p_new = jnp.exp(s_new - m)
    l = jnp.sum(p_past, axis=1, keepdims=True) + jnp.sum(p_new, axis=1, keepdims=True)
    p = jnp.concatenate([p_past, p_new], axis=1).astype(BF16)
    out = jnp.dot(p, vb_scr[...], preferred_element_type=F32) / l

    lam = _lambda(lam_ref, lam_init)
    heads = []
    for h in range(DA_HEADS):
        cols = slice(h * DA_DV, (h + 1) * DA_DV)
        o0 = out[(2 * h) * t_new:(2 * h + 1) * t_new, cols]
        o1 = out[(2 * h + 1) * t_new:(2 * h + 2) * t_new, cols]
        heads.append(_head_norm(o0 - lam * o1, nw_ref[...], 1.0 - lam_init))
    o_ref[0] = jnp.concatenate(heads, axis=1) * ga_ref[0]


def _attn_sample(page_table, q, k_new, v_new, ga, norm_w, lam_p, cache_kt, cache_v, layer, lam_init):
    nb, t_new, _ = q.shape
    tok = pl.BlockSpec((1, t_new, D_MODEL), lambda bi, pt: (bi, 0, 0))

    def page_spec(p):
        return pl.BlockSpec((None, None, D_MODEL, PAGE_SIZE),
                            lambda bi, pt: (layer, pt[bi * N_PAGES + p], 0, 0))

    pages = [page_spec(p) for p in range(N_PAGES)]
    return pl.pallas_call(
        functools.partial(_attn_sample_kernel, lam_init=lam_init, t_new=t_new),
        grid_spec=pltpu.PrefetchScalarGridSpec(
            num_scalar_prefetch=1,
            grid=(nb,),
            in_specs=[tok, tok, tok, tok,
                      pl.BlockSpec((1, LANES), lambda bi, pt: (0, 0)),
                      pl.BlockSpec((4, DA_DK), lambda bi, pt: (0, 0))] + pages + pages,
            out_specs=tok,
            scratch_shapes=[pltpu.VMEM((D_MODEL, N_PAGES * PAGE_SIZE), BF16),
                            pltpu.VMEM(((N_PAGES + 1) * PAGE_SIZE, D_MODEL), BF16)]),
        out_shape=jax.ShapeDtypeStruct((nb, t_new, D_MODEL), F32),
        compiler_params=_params("arbitrary"),
        name="attn_sample",
    )(page_table.reshape(-1), q, k_new, v_new, ga, norm_w, lam_p,
      *([cache_kt] * N_PAGES), *([cache_v] * N_PAGES))


def _split3(x):
    hi = x.astype(BF16)
    r1 = x - hi.astype(F32)
    mid = r1.astype(BF16)
    lo = (r1 - mid.astype(F32)).astype(BF16)
    return hi, mid, lo


def _gla_kernel(*refs, bb, t_blk, rows, chunk, has_s0):
    q_ref, k_ref, v_ref, la_ref, gb_ref, nw_ref = refs[:6]
    y_ref, sf_ref, s_scr = refs[-3:]
    blk = pl.program_id(1)
    n_chunks = rows // chunk
    chains = [(bi, h) for bi in range(bb) for h in range(GLA_HEADS)]

    @pl.when(blk == 0)
    def _():
        if has_s0:
            s_scr[...] = refs[6][...]
        else:
            s_scr[...] = jnp.zeros(s_scr.shape, F32)

    def load(ref, bi, lo, width):
        x = ref[bi, :, lo:lo + width]
        if t_blk < rows:
            x = jnp.concatenate([x, jnp.zeros((rows - t_blk, width), F32)], axis=0)
        return x

    r = lax.broadcasted_iota(jnp.int32, (rows, rows), 0)
    c = lax.broadcasted_iota(jnp.int32, (rows, rows), 1)
    tri = (r // chunk == c // chunk) & (c <= r)
    tri_b = tri.astype(BF16)

    q_dec, k_end, vb, o_intra, decay_t, s = [], [], [], [], [], []
    for bi, h in chains:
        q, k, la = (load(ref, bi, h * GLA_DK, GLA_DK) for ref in (q_ref, k_ref, la_ref))
        v = load(v_ref, bi, h * GLA_DV, GLA_DV)
        b = sum(jnp.dot(tri_b, part, preferred_element_type=F32) for part in _split3(la))
        qd = (q * jnp.exp(b) * (GLA_DK ** -0.5)).astype(BF16)
        k_dec = (k * jnp.exp(-b)).astype(BF16)
        a = lax.dot_general(qd, k_dec, NT_DIMS, preferred_element_type=F32)
        a = jnp.where(tri, a, 0.0)
        vb.append(v.astype(BF16))
        q_dec.append(qd)
        o_intra.append(jnp.dot(a.astype(BF16), vb[-1], preferred_element_type=F32))
        b3 = b.reshape(n_chunks, chunk, GLA_DK)
        b_last = b3[:, chunk - 1:chunk, :]
        k_end.append((k.reshape(n_chunks, chunk, GLA_DK) * jnp.exp(b_last - b3))
                     .reshape(rows, GLA_DK).astype(BF16))
        decay = jnp.exp(b_last.reshape(n_chunks, GLA_DK))
        if n_chunks == 1:
            decay = jnp.broadcast_to(decay, (SUBLANES, GLA_DK))
        decay_t.append(decay.T)
        s.append(s_scr[bi, h])

    outs = [[] for _ in chains]
    for ci in range(n_chunks):
        sl = slice(ci * chunk, (ci + 1) * chunk)
        for n in range(len(chains)):
            outs[n].append(jnp.dot(q_dec[n][sl], s[n].astype(BF16), preferred_element_type=F32))
            ds = lax.dot_general(k_end[n][sl], vb[n][sl], TN_DIMS, preferred_element_type=F32)
            s[n] = decay_t[n][:, ci:ci + 1] * s[n] + ds

    for n, (bi, h) in enumerate(chains):
        s_scr[bi, h] = s[n]
        o = o_intra[n] + (jnp.concatenate(outs[n], axis=0) if n_chunks > 1 else outs[n][0])
        y = o * lax.rsqrt(jnp.mean(o * o, axis=-1, keepdims=True) + NORM_EPS) * nw_ref[...]
        cols = slice(h * GLA_DV, (h + 1) * GLA_DV)
        y_ref[bi, :, cols] = y[:t_blk] * gb_ref[bi, :, cols]

    @pl.when(blk == pl.num_programs(1) - 1)
    def _():
        for n, (bi, h) in enumerate(chains):
            sf_ref[bi, h] = s[n]


def _gla(gq, gk, gv, la, gb, norm_w, state, layer, prev_states, t_blk, bb):
    b, t, _ = gq.shape
    rows = max(t_blk, GLA_CHUNK)
    assert rows // GLA_CHUNK in (1, SUBLANES) and b % bb == 0
    kspec = pl.BlockSpec((bb, t_blk, GLA_KEY), lambda bi, i: (bi, i, 0))
    vspec = pl.BlockSpec((bb, t_blk, D_MODEL), lambda bi, i: (bi, i, 0))
    sspec = pl.BlockSpec((None, bb, GLA_HEADS, GLA_DK, GLA_DV), lambda bi, i: (layer, bi, 0, 0, 0))
    in_specs = [kspec, kspec, vspec, kspec, vspec, pl.BlockSpec((1, GLA_DV), lambda bi, i: (0, 0))]
    args = [gq, gk, gv, la, gb, norm_w]
    if state is not None:
        in_specs.append(sspec)
        args.append(state)
    aliases = {}
    if prev_states is not None:
        aliases = {len(args): 1}
        in_specs.append(pl.BlockSpec(memory_space=pl.ANY))
        args.append(prev_states)
    return pl.pallas_call(
        functools.partial(_gla_kernel, bb=bb, t_blk=t_blk, rows=rows, chunk=GLA_CHUNK,
                          has_s0=state is not None),
        grid=(b // bb, t // t_blk),
        in_specs=in_specs,
        out_specs=[vspec, sspec],
        out_shape=[jax.ShapeDtypeStruct((b, t, D_MODEL), F32),
                   jax.ShapeDtypeStruct((DEPTH, b, GLA_HEADS, GLA_DK, GLA_DV), F32)],
        scratch_shapes=[pltpu.VMEM((bb, GLA_HEADS, GLA_DK, GLA_DV), F32)],
        input_output_aliases=aliases,
        compiler_params=_params("parallel", "arbitrary"),
        name="gla",
    )(*args)


def _out_proj_kernel(ya_ref, yb_ref, x_ref, w_ref, lw_ref, lb_ref, o_ref):
    merged = (ya_ref[...] + yb_ref[...]).astype(BF16)
    h = jnp.dot(merged, w_ref[...], preferred_element_type=F32)
    y = DEEPNORM_ALPHA * x_ref[...] + h
    mu = jnp.mean(y, axis=-1, keepdims=True)
    d = y - mu
    var = jnp.mean(d * d, axis=-1, keepdims=True)
    o_ref[...] = d * lax.rsqrt(var + NORM_EPS) * lw_ref[...] + lb_ref[...]


def _out_proj(ya, yb, x, w_out, ln_w, ln_b, tm):
    n = x.shape[0]
    row = lambda i: (i, 0)
    const = lambda i: (0, 0)
    return pl.pallas_call(
        _out_proj_kernel,
        grid=(n // tm,),
        in_specs=[pl.BlockSpec((tm, D_MODEL), row)] * 3
                 + [pl.BlockSpec((D_MODEL, D_MODEL), const),
                    pl.BlockSpec((1, D_MODEL), const), pl.BlockSpec((1, D_MODEL), const)],
        out_specs=pl.BlockSpec((tm, D_MODEL), row),
        out_shape=jax.ShapeDtypeStruct((n, D_MODEL), F32),
        compiler_params=_params("parallel"),
        name="out_proj",
    )(ya, yb, x, w_out, ln_w, ln_b)


def _rope_tables(pos):
    half = DA_DK // 2
    inv = ROPE_THETA ** (-jnp.arange(0, DA_DK, 2, dtype=F32) / DA_DK)
    ang = pos.astype(F32)[:, None] * inv[None, :]
    cos32, sin32 = jnp.cos(ang), jnp.sin(ang)
    cos = jnp.tile(cos32, (1, LANES // half))
    sin = jnp.tile(sin32, (1, LANES // half))
    first = (jnp.arange(LANES) % DA_DK) < half
    return (cos, jnp.where(first, -sin, 0.0), jnp.where(first, 0.0, sin)), (cos32.T, sin32.T)


def _layer_weights(layer, w_in, gla_w_a2, gla_b_a, w_out):
    w = w_in[layer]
    o = [0]
    for s in (1024, 1024, 1024, 1024, 512, 512, 1024, 1024, 16, 1024, 1024):
        o.append(o[-1] + s)
    seg = lambda i: w[:, o[i]:o[i + 1]]
    return dict(
        wqt=seg(0).T.astype(BF16), wkt=seg(1).T.astype(BF16), wvt=seg(2).T.astype(BF16),
        w_g=jnp.concatenate([seg(4), seg(5), seg(6)], axis=1).astype(BF16),
        w_gates=jnp.concatenate([seg(3), seg(9), seg(7), seg(10)], axis=1).astype(BF16),
        w_lr=jnp.pad(seg(8), ((0, 0), (0, LANES - GLA_RANK))).astype(BF16),
        w_a2=jnp.pad(gla_w_a2[layer], ((0, LANES - GLA_RANK), (0, 0))).astype(BF16),
        b_a=gla_b_a[layer][None, :],
        w_out=w_out[layer].astype(BF16))


def kernel(x_prompt, x_sample, cache_k, cache_v, state_gla, page_table, w_in,
           lam_q1, lam_k1, lam_q2, lam_k2, da_norm_w, gla_w_a2, gla_b_a, gla_norm_w,
           w_out, ln_w, ln_b):
    bp, tp, _ = x_prompt.shape
    bs, ts, _ = x_sample.shape
    ns = bs * ts
    tm = 512
    tm_s = min(tm, ns)
    tab_p, tabt_p = _rope_tables(jnp.arange(tp))
    tab_s, _ = _rope_tables(PAST_LEN + jnp.arange(ts))
    tab_s = tuple(jnp.tile(t, (tm_s // ts, 1)) for t in tab_s)
    xp = x_prompt.reshape(bp * tp, D_MODEL)
    xs = x_sample.reshape(ns, D_MODEL)
    n_pool = cache_k.shape[1]
    cache_kt = jnp.transpose(cache_k, (0, 1, 3, 4, 5, 2)).reshape(DEPTH, n_pool, D_MODEL, PAGE_SIZE)
    cache_v2 = cache_v.reshape(DEPTH, n_pool, PAGE_SIZE * DA_HEADS, DA_DV)

    ks_rows, vs_rows = [], []
    kv_p = sp_all = ss_all = None
    for layer in range(DEPTH):
        lam_init = 0.8 - 0.6 * math.exp(-0.3 * layer)
        w = _layer_weights(layer, w_in, gla_w_a2, gla_b_a, w_out)
        lam_p = jnp.stack([lam_q1[layer], lam_k1[layer], lam_q2[layer], lam_k2[layer]])
        da_w = da_norm_w[layer][None, :]
        gla_w = gla_norm_w[layer][None, :]
        lw, lb = ln_w[layer][None, :], ln_b[layer][None, :]
        sh3 = lambda a, b: a.reshape(b, -1, a.shape[-1])

        qt, kt_all, kb, vf_all, vtb = _proj_attn(sh3(xp, bp), w["wqt"], w["wkt"], w["wvt"], tab_p, tabt_p, tm,
                                                 True, layer, kv_p)
        kv_p = (kt_all, vf_all)
        gq_p, gk_p, gv_p, la_p = _proj_gla(xp, w["w_g"], w["w_lr"], w["w_a2"], w["b_a"], tm)
        ga_p, gb_p = _proj_gates(xp, w["w_gates"], tm)
        q, kf, vf = _proj_attn(sh3(xs, 1), w["wqt"], w["wkt"], w["wvt"], tab_s, None, tm_s, False)
        gq_s, gk_s, gv_s, la_s = _proj_gla(xs, w["w_g"], w["w_lr"], w["w_a2"], w["b_a"], tm_s)
        ga_s, gb_s = _proj_gates(xs, w["w_gates"], tm_s)
        ya_p = _attn_prompt(qt, kb, vtb, sh3(ga_p, bp), da_w, lam_p, lam_init, tq=tm, tk=tm, hg=1)
        ya_s = _attn_sample(page_table, sh3(q, bs), sh3(kf, bs), sh3(vf, bs), sh3(ga_s, bs), da_w, lam_p,
                            cache_kt, cache_v2, layer, lam_init)

        yb, sp_all = _gla(sh3(gq_p, bp), sh3(gk_p, bp), sh3(gv_p, bp), sh3(la_p, bp), sh3(gb_p, bp), gla_w,
                          None, layer, sp_all, 256, bp)
        xp = _out_proj(ya_p.reshape(-1, D_MODEL), yb.reshape(-1, D_MODEL), xp, w["w_out"], lw, lb, tm)
        yb, ss_all = _gla(sh3(gq_s, bs), sh3(gk_s, bs), sh3(gv_s, bs), sh3(la_s, bs), sh3(gb_s, bs), gla_w,
                          state_gla, layer, ss_all, ts, 8)
        xs = _out_proj(ya_s.reshape(-1, D_MODEL), yb.reshape(-1, D_MODEL), xs, w["w_out"], lw, lb, tm_s)
        ks_rows.append(kf.reshape(bs, ts, DA_HEADS, 2, DA_DK))
        vs_rows.append(vf.reshape(bs, ts, DA_HEADS, DA_DV))

    kt_all, vf_all = kv_p
    new_k_p = jnp.transpose(kt_all.reshape(DEPTH, bp, DA_HEADS, 2, DA_DK, tp), (0, 1, 5, 2, 3, 4))
    return (xp.reshape(bp, tp, D_MODEL), xs.reshape(bs, ts, D_MODEL),
            new_k_p, vf_all.reshape(DEPTH, bp, tp, DA_HEADS, DA_DV), sp_all,
            jnp.stack(ks_rows), jnp.stack(vs_rows), ss_all)
```

```python
import functools
import math

import jax
import jax.numpy as jnp
from jax import lax
from jax.experimental import pallas as pl
from jax.experimental.pallas import tpu as pltpu

D_MODEL = 1024
DEPTH = 2
PAST_LEN = 2048
PAGE_SIZE = 128
N_PAGES = PAST_LEN // PAGE_SIZE
DA_HEADS = 8
DA_DK = 64
DA_DV = 128
ROPE_THETA = 10000.0
GLA_HEADS = 4
GLA_KEY = 512
GLA_DK = 128
GLA_DV = 256
GLA_RANK = 16
GLA_TAU = 16.0
GLA_CHUNK = 32
NORM_EPS = 1e-5
DEEPNORM_ALPHA = (2.0 * DEPTH) ** 0.25
MASK_VALUE = -1e30
LOG2_E = math.log2(math.e)

LANES = 128
SUBLANES = 8
VMEM_LIMIT = 56 * 1024 * 1024

F32 = jnp.float32
BF16 = jnp.bfloat16
NT_DIMS = (((1,), (1,)), ((), ()))
TN_DIMS = (((0,), (0,)), ((), ()))


def _params(*sem):
    return pltpu.CompilerParams(dimension_semantics=sem, vmem_limit_bytes=VMEM_LIMIT)


def _sigmoid(x):
    return 1.0 / (1.0 + jnp.exp(-x))


def _rope_lanes(zj, cos, sina, sinb):
    return zj * cos + pltpu.roll(zj, LANES - 32, 1) * sina + pltpu.roll(zj, 32, 1) * sinb


def _rope_rows(zt, cost, sint):
    half = DA_DK // 2
    z4 = zt.reshape(2 * DA_HEADS, 2, half, zt.shape[1])
    x1, x2 = z4[:, 0], z4[:, 1]
    ct, st = cost[None], sint[None]
    return jnp.stack([x1 * ct - x2 * st, x2 * ct + x1 * st], axis=1).reshape(zt.shape)


def _rope_rows_token_major(z, cos, sina, sinb, store):
    for j in range(D_MODEL // LANES):
        sl = slice(j * LANES, (j + 1) * LANES)
        store(sl, _rope_lanes(z[:, sl], cos, sina, sinb))


def _proj_attn_prompt_kernel(x_ref, wqt_ref, wkt_ref, wvt_ref, cost_ref, sint_ref, *rest):
    qt_ref, kt_ref, kb_ref, vf_ref, vtb_ref = rest[-5:]
    xb = x_ref[0].astype(BF16)
    cost, sint = cost_ref[...], sint_ref[...]
    tn = lambda w_ref: lax.dot_general(w_ref[...], xb, NT_DIMS, preferred_element_type=F32)
    nt = lambda w_ref: lax.dot_general(xb, w_ref[...], NT_DIMS, preferred_element_type=F32)
    qt_ref[0] = (_rope_rows(tn(wqt_ref), cost, sint) * (DA_DK ** -0.5 * LOG2_E)).astype(BF16)
    kt = _rope_rows(tn(wkt_ref), cost, sint)
    kt_ref[0] = kt
    kb_ref[0] = kt.T.astype(BF16)
    v = nt(wvt_ref)
    vf_ref[0] = v
    vtb_ref[0, 0] = v.T.astype(BF16)


def _proj_attn_sample_kernel(x_ref, wqt_ref, wkt_ref, wvt_ref, cos_ref, sina_ref, sinb_ref,
                             q_ref, k_ref, v_ref):
    xb = x_ref[0].astype(BF16)
    cos, sina, sinb = cos_ref[...], sina_ref[...], sinb_ref[...]
    nt = lambda w_ref: lax.dot_general(xb, w_ref[...], NT_DIMS, preferred_element_type=F32)

    def store_q(sl, val):
        q_ref[0, :, sl] = (val * (DA_DK ** -0.5)).astype(BF16)

    def store_k(sl, val):
        k_ref[0, :, sl] = val
    _rope_rows_token_major(nt(wqt_ref), cos, sina, sinb, store_q)
    _rope_rows_token_major(nt(wkt_ref), cos, sina, sinb, store_k)
    v_ref[0] = nt(wvt_ref)


def _proj_attn(x, wqt, wkt, wvt, tables, tm, prompt, layer=0, prev_kv=None):
    b, t, _ = x.shape
    nt = tables[0].shape[1 if prompt else 0] // tm
    tab = lambda bi, i: (i % nt, 0)
    tab_t = lambda bi, i: (0, i % nt)
    wspec = pl.BlockSpec((D_MODEL, D_MODEL), lambda bi, i: (0, 0))
    tok = pl.BlockSpec((1, tm, D_MODEL), lambda bi, i: (bi, i, 0))
    tok_t = pl.BlockSpec((1, D_MODEL, tm), lambda bi, i: (bi, 0, i))
    tabs = [pl.BlockSpec((tm, LANES), tab)] * 3
    shape = lambda s, dt: jax.ShapeDtypeStruct(s, dt)
    aliases = {}
    if prompt:
        body = _proj_attn_prompt_kernel
        in_specs = [tok, wspec, wspec, wspec] + [pl.BlockSpec((DA_DK // 2, tm), tab_t)] * 2
        args = [x, wqt, wkt, wvt, *tables]
        if prev_kv is not None:
            aliases = {len(args): 1, len(args) + 1: 3}
            in_specs += [pl.BlockSpec(memory_space=pl.ANY)] * 2
            args += list(prev_kv)
        out_specs = [tok_t,
                     pl.BlockSpec((None, 1, D_MODEL, tm), lambda bi, i: (layer, bi, 0, i)),
                     tok,
                     pl.BlockSpec((None, 1, tm, D_MODEL), lambda bi, i: (layer, bi, i, 0)),
                     pl.BlockSpec((1, 1, D_MODEL, tm), lambda bi, i: (bi, i, 0, 0))]
        out_shape = [shape((b, D_MODEL, t), BF16), shape((DEPTH, b, D_MODEL, t), F32),
                     shape((b, t, D_MODEL), BF16), shape((DEPTH, b, t, D_MODEL), F32),
                     shape((b, t // tm, D_MODEL, tm), BF16)]
    else:
        body = _proj_attn_sample_kernel
        in_specs = [tok, wspec, wspec, wspec] + tabs
        args = [x, wqt, wkt, wvt, *tables]
        out_specs = [tok, tok, tok]
        out_shape = [shape((b, t, D_MODEL), BF16), shape((b, t, D_MODEL), F32), shape((b, t, D_MODEL), F32)]
    return pl.pallas_call(
        body, grid=(b, t // tm), in_specs=in_specs, out_specs=out_specs, out_shape=out_shape,
        input_output_aliases=aliases,
        compiler_params=_params("parallel", "parallel"),
        name="proj_attn_prompt" if prompt else "proj_attn_sample",
    )(*args)


def _proj_gla_kernel(x_ref, w_ref, wlr_ref, wa2_ref, ba_ref,
                     gq_ref, gk_ref, gv_ref, la_ref):
    xb = x_ref[...].astype(BF16)
    z = jnp.dot(xb, w_ref[...], preferred_element_type=F32)
    gq_ref[...] = z[:, :GLA_KEY]
    gk_ref[...] = z[:, GLA_KEY:2 * GLA_KEY]
    gv_ref[...] = z[:, 2 * GLA_KEY:]
    g_lr = jnp.dot(xb, wlr_ref[...], preferred_element_type=F32)
    logit = jnp.dot(g_lr.astype(BF16), wa2_ref[...], preferred_element_type=F32) + ba_ref[...]
    log_sig = jnp.minimum(logit, 0.0) - jnp.log1p(jnp.exp(-jnp.abs(logit)))
    la_ref[...] = log_sig / GLA_TAU


def _proj_gla(x, w_g, w_lr, w_a2, b_a, tm):
    n = x.shape[0]
    row = lambda i: (i, 0)
    const = lambda i: (0, 0)
    return pl.pallas_call(
        _proj_gla_kernel,
        grid=(n // tm,),
        in_specs=[pl.BlockSpec((tm, D_MODEL), row),
                  pl.BlockSpec((D_MODEL, 2 * D_MODEL), const),
                  pl.BlockSpec((D_MODEL, LANES), const),
                  pl.BlockSpec((LANES, GLA_KEY), const),
                  pl.BlockSpec((1, GLA_KEY), const)],
        out_specs=[pl.BlockSpec((tm, GLA_KEY), row), pl.BlockSpec((tm, GLA_KEY), row),
                   pl.BlockSpec((tm, D_MODEL), row), pl.BlockSpec((tm, GLA_KEY), row)],
        out_shape=[jax.ShapeDtypeStruct((n, GLA_KEY), F32),
                   jax.ShapeDtypeStruct((n, GLA_KEY), F32),
                   jax.ShapeDtypeStruct((n, D_MODEL), F32),
                   jax.ShapeDtypeStruct((n, GLA_KEY), F32)],
        compiler_params=_params("parallel"),
        name="proj_gla",
    )(x, w_g, w_lr, w_a2, b_a)


def _proj_gates_kernel(x_ref, w_ref, ga_ref, gb_ref):
    xb = x_ref[...].astype(BF16)
    z = jnp.dot(xb, w_ref[...], preferred_element_type=F32)
    dg, ma = z[:, :D_MODEL], z[:, D_MODEL:2 * D_MODEL]
    gg, mb = z[:, 2 * D_MODEL:3 * D_MODEL], z[:, 3 * D_MODEL:]
    ga_ref[...] = _sigmoid(ma) * (dg * _sigmoid(dg))
    gb_ref[...] = _sigmoid(mb) * (gg * _sigmoid(gg))


def _proj_gates(x, w_gates, tm):
    n = x.shape[0]
    row = lambda i: (i, 0)
    return pl.pallas_call(
        _proj_gates_kernel,
        grid=(n // tm,),
        in_specs=[pl.BlockSpec((tm, D_MODEL), row),
                  pl.BlockSpec((D_MODEL, 4 * D_MODEL), lambda i: (0, 0))],
        out_specs=[pl.BlockSpec((tm, D_MODEL), row)] * 2,
        out_shape=[jax.ShapeDtypeStruct((n, D_MODEL), F32)] * 2,
        compiler_params=_params("parallel"),
        name="proj_gates",
    )(x, w_gates)


def _lambda(lam_ref, lam_init):
    lp = lam_ref[...]
    a = jnp.sum(lp[0:1] * lp[1:2], axis=1, keepdims=True)
    b = jnp.sum(lp[2:3] * lp[3:4], axis=1, keepdims=True)
    return jnp.exp(a) - jnp.exp(b) + lam_init


def _head_norm(o, w, gain):
    return o * lax.rsqrt(jnp.mean(o * o, axis=-1, keepdims=True) + NORM_EPS) * w * gain


def _attn_prompt_kernel(qt_ref, k_ref, vt_ref, ga_ref, nw_ref, lam_ref, o_ref,
                        qq_scr, sa_scr, sb_scr, m_scr, l_scr, acc_scr, *, hg, tq, tk, lam_init):
    i = pl.program_id(2)
    row = lax.broadcasted_iota(jnp.int32, (LANES, tq), 0)
    for g in range(hg):
        qt = qt_ref[0, g * LANES:(g + 1) * LANES, :]
        zero = jnp.zeros_like(qt)
        qq_scr[g, :, :tq] = jnp.where(row < DA_DK, qt, zero)
        qq_scr[g, :, tq:] = jnp.where(row >= DA_DK, qt, zero)
    m_scr[...] = jnp.full(m_scr.shape, -jnp.inf, F32)
    l_scr[...] = jnp.zeros(l_scr.shape, F32)
    acc_scr[...] = jnp.zeros(acc_scr.shape, F32)

    def scores(j, s_ref):
        for g in range(hg):
            k = k_ref[0, pl.ds(pl.multiple_of(j * tk, tk), tk), g * LANES:(g + 1) * LANES]
            s_ref[g] = jnp.dot(k, qq_scr[g], preferred_element_type=F32)

    def softmax_pv(j, s_ref, masked):
        for g in range(hg):
            s = s_ref[g]
            if masked:
                r = lax.broadcasted_iota(jnp.int32, (tk, 2 * tq), 0)
                c = lax.broadcasted_iota(jnp.int32, (tk, 2 * tq), 1)
                qpos = i * tq + jnp.where(c >= tq, c - tq, c)
                s = jnp.where(j * tk + r <= qpos, s, MASK_VALUE)
            m_old = m_scr[g]
            m_new = jnp.maximum(m_old, jnp.max(s, axis=0, keepdims=True))
            alpha = jnp.exp2(m_old - m_new)
            p = jnp.exp2(s - m_new)
            l_scr[g] = alpha * l_scr[g] + jnp.sum(p, axis=0, keepdims=True)
            vt = vt_ref[0, j, g * LANES:(g + 1) * LANES, :]
            acc_scr[g] = alpha * acc_scr[g] + jnp.dot(vt, p.astype(BF16), preferred_element_type=F32)
            m_scr[g] = m_new

    assert tq == tk
    scores(0, sa_scr)

    def body(t, carry):
        scores(2 * t + 1, sb_scr)
        softmax_pv(2 * t, sa_scr, False)
        scores(2 * t + 2, sa_scr)
        softmax_pv(2 * t + 1, sb_scr, False)
        return carry
    lax.fori_loop(0, i // 2, body, 0)

    @pl.when(i % 2 == 1)
    def _():
        scores(i, sb_scr)
        softmax_pv(i - 1, sa_scr, False)
        softmax_pv(i, sb_scr, True)

    @pl.when(i % 2 == 0)
    def _():
        softmax_pv(i, sa_scr, True)

    lam = _lambda(lam_ref, lam_init)
    for g in range(hg):
        ot = acc_scr[g] / l_scr[g]
        o = (ot[:, :tq] - lam * ot[:, tq:]).T
        cols = slice(g * LANES, (g + 1) * LANES)
        o_ref[0, :, cols] = _head_norm(o, nw_ref[...], 1.0 - lam_init) * ga_ref[0, :, cols]


def _attn_prompt(qt, k, vt, ga, norm_w, lam_p, lam_init, tq, tk, hg):
    b, t, _ = k.shape
    nk = t // tk
    assert vt.shape == (b, nk, D_MODEL, tk) and DA_HEADS % hg == 0
    w = hg * LANES
    blk = pl.BlockSpec((1, tq, w), lambda bi, h, i: (bi, i, h))
    return pl.pallas_call(
        functools.partial(_attn_prompt_kernel, hg=hg, tq=tq, tk=tk, lam_init=lam_init),
        grid=(b, DA_HEADS // hg, t // tq),
        in_specs=[pl.BlockSpec((1, w, tq), lambda bi, h, i: (bi, h, i)),
                  pl.BlockSpec((1, t, w), lambda bi, h, i: (bi, 0, h)),
                  pl.BlockSpec((1, nk, w, tk), lambda bi, h, i: (bi, 0, h, 0)),
                  blk,
                  pl.BlockSpec((1, LANES), lambda bi, h, i: (0, 0)),
                  pl.BlockSpec((4, DA_DK), lambda bi, h, i: (0, 0))],
        out_specs=blk,
        out_shape=jax.ShapeDtypeStruct((b, t, D_MODEL), F32),
        scratch_shapes=[pltpu.VMEM((hg, LANES, 2 * tq), BF16),
                        pltpu.VMEM((hg, tk, 2 * tq), F32),
                        pltpu.VMEM((hg, tk, 2 * tq), F32),
                        pltpu.VMEM((hg, 1, 2 * tq), F32),
                        pltpu.VMEM((hg, 1, 2 * tq), F32),
                        pltpu.VMEM((hg, LANES, 2 * tq), F32)],
        compiler_params=_params("parallel", "parallel", "arbitrary"),
        name="attn_prompt",
    )(qt, k, vt, ga, norm_w, lam_p)


def _attn_sample_kernel(pt_ref, q_ref, kn_ref, vn_ref, ga_ref, nw_ref, lam_ref, *rest,
                        lam_init, t_new):
    k_pages = rest[:N_PAGES]
    v_pages = rest[N_PAGES:2 * N_PAGES]
    o_ref, kb_scr, vb_scr = rest[2 * N_PAGES:]
    n_rows = 2 * DA_HEADS * t_new
    past = N_PAGES * PAGE_SIZE

    for p in range(N_PAGES):
        kb_scr[:, p * PAGE_SIZE:(p + 1) * PAGE_SIZE] = k_pages[p][...].astype(BF16)
        for h in range(DA_HEADS):
            vh = v_pages[p][pl.ds(h, PAGE_SIZE, stride=DA_HEADS), :]
            vb_scr[p * PAGE_SIZE:(p + 1) * PAGE_SIZE, h * DA_DV:(h + 1) * DA_DV] = vh.astype(BF16)
    pad = jnp.zeros((PAGE_SIZE - t_new, D_MODEL), F32)
    k_new = jnp.concatenate([kn_ref[0], pad], axis=0).astype(BF16)
    vb_scr[past:, :] = jnp.concatenate([vn_ref[0], pad], axis=0).astype(BF16)

    qf = q_ref[0].astype(F32)
    q_rep = jnp.concatenate([qf] * (2 * DA_HEADS), axis=0)
    r = lax.broadcasted_iota(jnp.int32, (n_rows, D_MODEL), 0)
    c = lax.broadcasted_iota(jnp.int32, (n_rows, D_MODEL), 1)
    q_exp = jnp.where(c // DA_DK == r // t_new, q_rep, 0.0).astype(BF16)

    s_past = jnp.dot(q_exp, kb_scr[...], preferred_element_type=F32)
    s_new = lax.dot_general(q_exp, k_new, NT_DIMS, preferred_element_type=F32)
    rr = lax.broadcasted_iota(jnp.int32, s_new.shape, 0)
    cc = lax.broadcasted_iota(jnp.int32, s_new.shape, 1)
    s_new = jnp.where(cc <= rr % t_new, s_new, MASK_VALUE)
    m = jnp.maximum(jnp.max(s_past, axis=1, keepdims=True), jnp.max(s_new, axis=1, keepdims=True))
    p_past = jnp.exp(s_past - m)
    p_new = jnp.exp(s_new - m)
    l = jnp.sum(p_past, axis=1, keepdims=True) + jnp.sum(p_new, axis=1, keepdims=True)
    p = jnp.concatenate([p_past, p_new], axis=1).astype(BF16)
    out = jnp.dot(p, vb_scr[...], preferred_element_type=F32) / l

    lam = _lambda(lam_ref, lam_init)
    heads = []
    for h in range(DA_HEADS):
        cols = slice(h * DA_DV, (h + 1) * DA_DV)
        o0 = out[(2 * h) * t_new:(2 * h + 1) * t_new, cols]
        o1 = out[(2 * h + 1) * t_new:(2 * h + 2) * t_new, cols]
        heads.append(_head_norm(o0 - lam * o1, nw_ref[...], 1.0 - lam_init))
    o_ref[0] = jnp.concatenate(heads, axis=1) * ga_ref[0]


def _attn_sample(page_table, q, k_new, v_new, ga, norm_w, lam_p, cache_kt, cache_v, layer, lam_init):
    nb, t_new, _ = q.shape
    tok = pl.BlockSpec((1, t_new, D_MODEL), lambda bi, pt: (bi, 0, 0))

    def page_spec(p):
        return pl.BlockSpec((None, None, D_MODEL, PAGE_SIZE),
                            lambda bi, pt: (layer, pt[bi * N_PAGES + p], 0, 0))

    pages = [page_spec(p) for p in range(N_PAGES)]
    return pl.pallas_call(
        functools.partial(_attn_sample_kernel, lam_init=lam_init, t_new=t_new),
        grid_spec=pltpu.PrefetchScalarGridSpec(
            num_scalar_prefetch=1,
            grid=(nb,),
            in_specs=[tok, tok, tok, tok,
                      pl.BlockSpec((1, LANES), lambda bi, pt: (0, 0)),
                      pl.BlockSpec((4, DA_DK), lambda bi, pt: (0, 0))] + pages + pages,
            out_specs=tok,
            scratch_shapes=[pltpu.VMEM((D_MODEL, N_PAGES * PAGE_SIZE), BF16),
                            pltpu.VMEM(((N_PAGES + 1) * PAGE_SIZE, D_MODEL), BF16)]),
        out_shape=jax.ShapeDtypeStruct((nb, t_new, D_MODEL), F32),
        compiler_params=_params("arbitrary"),
        name="attn_sample",
    )(page_table.reshape(-1), q, k_new, v_new, ga, norm_w, lam_p,
      *([cache_kt] * N_PAGES), *([cache_v] * N_PAGES))


def _split3(x):
    hi = x.astype(BF16)
    r1 = x - hi.astype(F32)
    mid = r1.astype(BF16)
    lo = (r1 - mid.astype(F32)).astype(BF16)
    return hi, mid, lo


def _gla_kernel(*refs, bb, t_blk, rows, chunk, has_s0):
    q_ref, k_ref, v_ref, la_ref, gb_ref, nw_ref = refs[:6]
    y_ref, sf_ref, s_scr = refs[-3:]
    blk = pl.program_id(1)
    n_chunks = rows // chunk
    chains = [(bi, h) for bi in range(bb) for h in range(GLA_HEADS)]

    @pl.when(blk == 0)
    def _():
        if has_s0:
            s_scr[...] = refs[6][...]
        else:
            s_scr[...] = jnp.zeros(s_scr.shape, F32)

    def load(ref, bi, lo, width):
        x = ref[bi, :, lo:lo + width]
        if t_blk < rows:
            x = jnp.concatenate([x, jnp.zeros((rows - t_blk, width), F32)], axis=0)
        return x

    r = lax.broadcasted_iota(jnp.int32, (rows, rows), 0)
    c = lax.broadcasted_iota(jnp.int32, (rows, rows), 1)
    tri = (r // chunk == c // chunk) & (c <= r)
    tri_b = tri.astype(BF16)

    q_dec, k_end, vb, o_intra, decay_t, s = [], [], [], [], [], []
    for bi, h in chains:
        q, k, la = (load(ref, bi, h * GLA_DK, GLA_DK) for ref in (q_ref, k_ref, la_ref))
        v = load(v_ref, bi, h * GLA_DV, GLA_DV)
        b = sum(jnp.dot(tri_b, part, preferred_element_type=F32) for part in _split3(la))
        qd = (q * jnp.exp(b) * (GLA_DK ** -0.5)).astype(BF16)
        k_dec = (k * jnp.exp(-b)).astype(BF16)
        a = lax.dot_general(qd, k_dec, NT_DIMS, preferred_element_type=F32)
        a = jnp.where(tri, a, 0.0)
        vb.append(v.astype(BF16))
        q_dec.append(qd)
        o_intra.append(jnp.dot(a.astype(BF16), vb[-1], preferred_element_type=F32))
        b3 = b.reshape(n_chunks, chunk, GLA_DK)
        b_last = b3[:, chunk - 1:chunk, :]
        k_end.append((k.reshape(n_chunks, chunk, GLA_DK) * jnp.exp(b_last - b3))
                     .reshape(rows, GLA_DK).astype(BF16))
        decay = jnp.exp(b_last.reshape(n_chunks, GLA_DK))
        if n_chunks == 1:
            decay = jnp.broadcast_to(decay, (SUBLANES, GLA_DK))
        decay_t.append(decay.T)
        s.append(s_scr[bi, h])

    outs = [[] for _ in chains]
    for ci in range(n_chunks):
        sl = slice(ci * chunk, (ci + 1) * chunk)
        for n in range(len(chains)):
            outs[n].append(jnp.dot(q_dec[n][sl], s[n].astype(BF16), preferred_element_type=F32))
            ds = lax.dot_general(k_end[n][sl], vb[n][sl], TN_DIMS, preferred_element_type=F32)
            s[n] = decay_t[n][:, ci:ci + 1] * s[n] + ds

    for n, (bi, h) in enumerate(chains):
        s_scr[bi, h] = s[n]
        o = o_intra[n] + (jnp.concatenate(outs[n], axis=0) if n_chunks > 1 else outs[n][0])
        y = o * lax.rsqrt(jnp.mean(o * o, axis=-1, keepdims=True) + NORM_EPS) * nw_ref[...]
        cols = slice(h * GLA_DV, (h + 1) * GLA_DV)
        y_ref[bi, :, cols] = y[:t_blk] * gb_ref[bi, :, cols]

    @pl.when(blk == pl.num_programs(1) - 1)
    def _():
        for n, (bi, h) in enumerate(chains):
            sf_ref[bi, h] = s[n]


def _gla(gq, gk, gv, la, gb, norm_w, state, layer, prev_states, t_blk, bb):
    b, t, _ = gq.shape
    rows = max(t_blk, GLA_CHUNK)
    assert rows // GLA_CHUNK in (1, SUBLANES) and b % bb == 0
    kspec = pl.BlockSpec((bb, t_blk, GLA_KEY), lambda bi, i: (bi, i, 0))
    vspec = pl.BlockSpec((bb, t_blk, D_MODEL), lambda bi, i: (bi, i, 0))
    sspec = pl.BlockSpec((None, bb, GLA_HEADS, GLA_DK, GLA_DV), lambda bi, i: (layer, bi, 0, 0, 0))
    in_specs = [kspec, kspec, vspec, kspec, vspec, pl.BlockSpec((1, GLA_DV), lambda bi, i: (0, 0))]
    args = [gq, gk, gv, la, gb, norm_w]
    if state is not None:
        in_specs.append(sspec)
        args.append(state)
    aliases = {}
    if prev_states is not None:
        aliases = {len(args): 1}
        in_specs.append(pl.BlockSpec(memory_space=pl.ANY))
        args.append(prev_states)
    return pl.pallas_call(
        functools.partial(_gla_kernel, bb=bb, t_blk=t_blk, rows=rows, chunk=GLA_CHUNK,
                          has_s0=state is not None),
        grid=(b // bb, t // t_blk),
        in_specs=in_specs,
        out_specs=[vspec, sspec],
        out_shape=[jax.ShapeDtypeStruct((b, t, D_MODEL), F32),
                   jax.ShapeDtypeStruct((DEPTH, b, GLA_HEADS, GLA_DK, GLA_DV), F32)],
        scratch_shapes=[pltpu.VMEM((bb, GLA_HEADS, GLA_DK, GLA_DV), F32)],
        input_output_aliases=aliases,
        compiler_params=_params("parallel", "arbitrary"),
        name="gla",
    )(*args)


def _out_proj_kernel(ya_ref, yb_ref, x_ref, w_ref, lw_ref, lb_ref, o_ref):
    merged = (ya_ref[...] + yb_ref[...]).astype(BF16)
    h = jnp.dot(merged, w_ref[...], preferred_element_type=F32)
    y = DEEPNORM_ALPHA * x_ref[...] + h
    mu = jnp.mean(y, axis=-1, keepdims=True)
    d = y - mu
    var = jnp.mean(d * d, axis=-1, keepdims=True)
    o_ref[...] = d * lax.rsqrt(var + NORM_EPS) * lw_ref[...] + lb_ref[...]


def _out_proj(ya, yb, x, w_out, ln_w, ln_b, tm):
    n = x.shape[0]
    row = lambda i: (i, 0)
    const = lambda i: (0, 0)
    return pl.pallas_call(
        _out_proj_kernel,
        grid=(n // tm,),
        in_specs=[pl.BlockSpec((tm, D_MODEL), row)] * 3
                 + [pl.BlockSpec((D_MODEL, D_MODEL), const),
                    pl.BlockSpec((1, D_MODEL), const), pl.BlockSpec((1, D_MODEL), const)],
        out_specs=pl.BlockSpec((tm, D_MODEL), row),
        out_shape=jax.ShapeDtypeStruct((n, D_MODEL), F32),
        compiler_params=_params("parallel"),
        name="out_proj",
    )(ya, yb, x, w_out, ln_w, ln_b)


def _rope_tables(pos):
    half = DA_DK // 2
    inv = ROPE_THETA ** (-jnp.arange(0, DA_DK, 2, dtype=F32) / DA_DK)
    ang = pos.astype(F32)[:, None] * inv[None, :]
    cos32, sin32 = jnp.cos(ang), jnp.sin(ang)
    cos = jnp.tile(cos32, (1, LANES // half))
    sin = jnp.tile(sin32, (1, LANES // half))
    first = (jnp.arange(LANES) % DA_DK) < half
    return (cos, jnp.where(first, -sin, 0.0), jnp.where(first, 0.0, sin)), (cos32.T, sin32.T)


def _layer_weights(layer, w_in, gla_w_a2, gla_b_a, w_out):
    w = w_in[layer]
    o = [0]
    for s in (1024, 1024, 1024, 1024, 512, 512, 1024, 1024, 16, 1024, 1024):
        o.append(o[-1] + s)
    seg = lambda i: w[:, o[i]:o[i + 1]]
    return dict(
        wqt=seg(0).T.astype(BF16), wkt=seg(1).T.astype(BF16), wvt=seg(2).T.astype(BF16),
        w_g=jnp.concatenate([seg(4), seg(5), seg(6)], axis=1).astype(BF16),
        w_gates=jnp.concatenate([seg(3), seg(9), seg(7), seg(10)], axis=1).astype(BF16),
        w_lr=jnp.pad(seg(8), ((0, 0), (0, LANES - GLA_RANK))).astype(BF16),
        w_a2=jnp.pad(gla_w_a2[layer], ((0, LANES - GLA_RANK), (0, 0))).astype(BF16),
        b_a=gla_b_a[layer][None, :],
        w_out=w_out[layer].astype(BF16))


def kernel(x_prompt, x_sample, cache_k, cache_v, state_gla, page_table, w_in,
           lam_q1, lam_k1, lam_q2, lam_k2, da_norm_w, gla_w_a2, gla_b_a, gla_norm_w,
           w_out, ln_w, ln_b):
    bp, tp, _ = x_prompt.shape
    bs, ts, _ = x_sample.shape
    ns = bs * ts
    tm = 512
    tm_s = min(tm, ns)
    _, tabt_p = _rope_tables(jnp.arange(tp))
    tab_s, _ = _rope_tables(PAST_LEN + jnp.arange(ts))
    tab_s = tuple(jnp.tile(t, (tm_s // ts, 1)) for t in tab_s)
    xp = x_prompt.reshape(bp * tp, D_MODEL)
    xs = x_sample.reshape(ns, D_MODEL)
    n_pool = cache_k.shape[1]
    cache_kt = jnp.transpose(cache_k, (0, 1, 3, 4, 5, 2)).reshape(DEPTH, n_pool, D_MODEL, PAGE_SIZE)
    cache_v2 = cache_v.reshape(DEPTH, n_pool, PAGE_SIZE * DA_HEADS, DA_DV)

    ks_rows, vs_rows = [], []
    kv_p = sp_all = ss_all = None
    for layer in range(DEPTH):
        lam_init = 0.8 - 0.6 * math.exp(-0.3 * layer)
        w = _layer_weights(layer, w_in, gla_w_a2, gla_b_a, w_out)
        lam_p = jnp.stack([lam_q1[layer], lam_k1[layer], lam_q2[layer], lam_k2[layer]])
        da_w = da_norm_w[layer][None, :]
        gla_w = gla_norm_w[layer][None, :]
        lw, lb = ln_w[layer][None, :], ln_b[layer][None, :]
        sh3 = lambda a, b: a.reshape(b, -1, a.shape[-1])

        qt, kt_all, kb, vf_all, vtb = _proj_attn(sh3(xp, bp), w["wqt"], w["wkt"], w["wvt"], tabt_p, tm,
                                                 True, layer, kv_p)
        kv_p = (kt_all, vf_all)
        gq_p, gk_p, gv_p, la_p = _proj_gla(xp, w["w_g"], w["w_lr"], w["w_a2"], w["b_a"], tm)
        ga_p, gb_p = _proj_gates(xp, w["w_gates"], tm)
        q, kf, vf = _proj_attn(sh3(xs, 1), w["wqt"], w["wkt"], w["wvt"], tab_s, tm_s, False)
        gq_s, gk_s, gv_s, la_s = _proj_gla(xs, w["w_g"], w["w_lr"], w["w_a2"], w["b_a"], tm_s)
        ga_s, gb_s = _proj_gates(xs, w["w_gates"], tm_s)
        ya_p = _attn_prompt(qt, kb, vtb, sh3(ga_p, bp), da_w, lam_p, lam_init, tq=tm, tk=tm, hg=1)
        ya_s = _attn_sample(page_table, sh3(q, bs), sh3(kf, bs), sh3(vf, bs), sh3(ga_s, bs), da_w, lam_p,
                            cache_kt, cache_v2, layer, lam_init)

        yb, sp_all = _gla(sh3(gq_p, bp), sh3(gk_p, bp), sh3(gv_p, bp), sh3(la_p, bp), sh3(gb_p, bp), gla_w,
                          None, layer, sp_all, 256, bp)
        xp = _out_proj(ya_p.reshape(-1, D_MODEL), yb.reshape(-1, D_MODEL), xp, w["w_out"], lw, lb, tm)
        yb, ss_all = _gla(sh3(gq_s, bs), sh3(gk_s, bs), sh3(gv_s, bs), sh3(la_s, bs), sh3(gb_s, bs), gla_w,
                          state_gla, layer, ss_all, ts, 8)
        xs = _out_proj(ya_s.reshape(-1, D_MODEL), yb.reshape(-1, D_MODEL), xs, w["w_out"], lw, lb, tm_s)
        ks_rows.append(kf.reshape(bs, ts, DA_HEADS, 2, DA_DK))
        vs_rows.append(vf.reshape(bs, ts, DA_HEADS, DA_DV))

    kt_all, vf_all = kv_p
    new_k_p = jnp.transpose(kt_all.reshape(DEPTH, bp, DA_HEADS, 2, DA_DK, tp), (0, 1, 5, 2, 3, 4))
    return (xp.reshape(bp, tp, D_MODEL), xs.reshape(bs, ts, D_MODEL),
            new_k_p, vf_all.reshape(DEPTH, bp, tp, DA_HEADS, DA_DV), sp_all,
            jnp.stack(ks_rows), jnp.stack(vs_rows), ss_all)
```

```python
import functools
import math

import jax
import jax.numpy as jnp
from jax import lax
from jax.experimental import pallas as pl
from jax.experimental.pallas import tpu as pltpu

D_MODEL = 1024
DEPTH = 2
PAST_LEN = 2048
PAGE_SIZE = 128
N_PAGES = PAST_LEN // PAGE_SIZE
DA_HEADS = 8
DA_DK = 64
DA_DV = 128
ROPE_THETA = 10000.0
GLA_HEADS = 4
GLA_KEY = 512
GLA_DK = 128
GLA_DV = 256
GLA_RANK = 16
GLA_TAU = 16.0
GLA_CHUNK = 32
NORM_EPS = 1e-5
DEEPNORM_ALPHA = (2.0 * DEPTH) ** 0.25
MASK_VALUE = -1e30
LOG2_E = math.log2(math.e)

LANES = 128
SUBLANES = 8
VMEM_LIMIT = 56 * 1024 * 1024

F32 = jnp.float32
BF16 = jnp.bfloat16
NT_DIMS = (((1,), (1,)), ((), ()))
TN_DIMS = (((0,), (0,)), ((), ()))


def _params(*sem):
    return pltpu.CompilerParams(dimension_semantics=sem, vmem_limit_bytes=VMEM_LIMIT)


def _sigmoid(x):
    return 1.0 / (1.0 + jnp.exp(-x))


def _rope_lanes(zj, cos, sina, sinb):
    return zj * cos + pltpu.roll(zj, LANES - 32, 1) * sina + pltpu.roll(zj, 32, 1) * sinb


def _rope_rows(zt, cost, sint):
    half = DA_DK // 2
    z4 = zt.reshape(2 * DA_HEADS, 2, half, zt.shape[1])
    x1, x2 = z4[:, 0], z4[:, 1]
    ct, st = cost[None], sint[None]
    return jnp.stack([x1 * ct - x2 * st, x2 * ct + x1 * st], axis=1).reshape(zt.shape)


def _rope_rows_token_major(z, cos, sina, sinb, store):
    for j in range(D_MODEL // LANES):
        sl = slice(j * LANES, (j + 1) * LANES)
        store(sl, _rope_lanes(z[:, sl], cos, sina, sinb))


def _proj_attn_prompt_kernel(x_ref, wqt_ref, wkt_ref, wvt_ref, cost_ref, sint_ref, *rest):
    qt_ref, kt_ref, kb_ref, vf_ref, vtb_ref = rest[-5:]
    xb = x_ref[0].astype(BF16)
    cost, sint = cost_ref[...], sint_ref[...]
    tn = lambda w_ref: lax.dot_general(w_ref[...], xb, NT_DIMS, preferred_element_type=F32)
    nt = lambda w_ref: lax.dot_general(xb, w_ref[...], NT_DIMS, preferred_element_type=F32)
    qt_ref[0] = (_rope_rows(tn(wqt_ref), cost, sint) * (DA_DK ** -0.5 * LOG2_E)).astype(BF16)
    kt = _rope_rows(tn(wkt_ref), cost, sint)
    kt_ref[0] = kt
    kb_ref[0] = kt.T.astype(BF16)
    v = nt(wvt_ref)
    vf_ref[0] = v
    vtb_ref[0, 0] = v.T.astype(BF16)


def _proj_attn_sample_kernel(x_ref, wqt_ref, wkt_ref, wvt_ref, cos_ref, sina_ref, sinb_ref,
                             q_ref, k_ref, v_ref):
    xb = x_ref[0].astype(BF16)
    cos, sina, sinb = cos_ref[...], sina_ref[...], sinb_ref[...]
    nt = lambda w_ref: lax.dot_general(xb, w_ref[...], NT_DIMS, preferred_element_type=F32)

    def store_q(sl, val):
        q_ref[0, :, sl] = (val * (DA_DK ** -0.5)).astype(BF16)

    def store_k(sl, val):
        k_ref[0, :, sl] = val
    _rope_rows_token_major(nt(wqt_ref), cos, sina, sinb, store_q)
    _rope_rows_token_major(nt(wkt_ref), cos, sina, sinb, store_k)
    v_ref[0] = nt(wvt_ref)


def _proj_attn(x, wqt, wkt, wvt, tables, tm, prompt, layer=0, prev_kv=None):
    b, t, _ = x.shape
    nt = tables[0].shape[1 if prompt else 0] // tm
    tab = lambda bi, i: (i % nt, 0)
    tab_t = lambda bi, i: (0, i % nt)
    wspec = pl.BlockSpec((D_MODEL, D_MODEL), lambda bi, i: (0, 0))
    tok = pl.BlockSpec((1, tm, D_MODEL), lambda bi, i: (bi, i, 0))
    tok_t = pl.BlockSpec((1, D_MODEL, tm), lambda bi, i: (bi, 0, i))
    tabs = [pl.BlockSpec((tm, LANES), tab)] * 3
    shape = lambda s, dt: jax.ShapeDtypeStruct(s, dt)
    aliases = {}
    if prompt:
        body = _proj_attn_prompt_kernel
        in_specs = [tok, wspec, wspec, wspec] + [pl.BlockSpec((DA_DK // 2, tm), tab_t)] * 2
        args = [x, wqt, wkt, wvt, *tables]
        if prev_kv is not None:
            aliases = {len(args): 1, len(args) + 1: 3}
            in_specs += [pl.BlockSpec(memory_space=pl.ANY)] * 2
            args += list(prev_kv)
        out_specs = [tok_t,
                     pl.BlockSpec((None, 1, D_MODEL, tm), lambda bi, i: (layer, bi, 0, i)),
                     tok,
                     pl.BlockSpec((None, 1, tm, D_MODEL), lambda bi, i: (layer, bi, i, 0)),
                     pl.BlockSpec((1, 1, D_MODEL, tm), lambda bi, i: (bi, i, 0, 0))]
        out_shape = [shape((b, D_MODEL, t), BF16), shape((DEPTH, b, D_MODEL, t), F32),
                     shape((b, t, D_MODEL), BF16), shape((DEPTH, b, t, D_MODEL), F32),
                     shape((b, t // tm, D_MODEL, tm), BF16)]
    else:
        body = _proj_attn_sample_kernel
        in_specs = [tok, wspec, wspec, wspec] + tabs
        args = [x, wqt, wkt, wvt, *tables]
        out_specs = [tok, tok, tok]
        out_shape = [shape((b, t, D_MODEL), BF16), shape((b, t, D_MODEL), F32), shape((b, t, D_MODEL), F32)]
    return pl.pallas_call(
        body, grid=(b, t // tm), in_specs=in_specs, out_specs=out_specs, out_shape=out_shape,
        input_output_aliases=aliases,
        compiler_params=_params("parallel", "parallel"),
        name="proj_attn_prompt" if prompt else "proj_attn_sample",
    )(*args)


def _proj_gla_kernel(x_ref, w_ref, wlr_ref, wa2_ref, ba_ref,
                     gq_ref, gk_ref, gv_ref, la_ref):
    xb = x_ref[...].astype(BF16)
    z = jnp.dot(xb, w_ref[...], preferred_element_type=F32)
    gq_ref[...] = z[:, :GLA_KEY]
    gk_ref[...] = z[:, GLA_KEY:2 * GLA_KEY]
    gv_ref[...] = z[:, 2 * GLA_KEY:]
    g_lr = jnp.dot(xb, wlr_ref[...], preferred_element_type=F32)
    logit = jnp.dot(g_lr.astype(BF16), wa2_ref[...], preferred_element_type=F32) + ba_ref[...]
    log_sig = jnp.minimum(logit, 0.0) - jnp.log1p(jnp.exp(-jnp.abs(logit)))
    la_ref[...] = log_sig / GLA_TAU


def _proj_gla(x, w_g, w_lr, w_a2, b_a, tm):
    n = x.shape[0]
    row = lambda i: (i, 0)
    const = lambda i: (0, 0)
    return pl.pallas_call(
        _proj_gla_kernel,
        grid=(n // tm,),
        in_specs=[pl.BlockSpec((tm, D_MODEL), row),
                  pl.BlockSpec((D_MODEL, 2 * D_MODEL), const),
                  pl.BlockSpec((D_MODEL, LANES), const),
                  pl.BlockSpec((LANES, GLA_KEY), const),
                  pl.BlockSpec((1, GLA_KEY), const)],
        out_specs=[pl.BlockSpec((tm, GLA_KEY), row), pl.BlockSpec((tm, GLA_KEY), row),
                   pl.BlockSpec((tm, D_MODEL), row), pl.BlockSpec((tm, GLA_KEY), row)],
        out_shape=[jax.ShapeDtypeStruct((n, GLA_KEY), F32),
                   jax.ShapeDtypeStruct((n, GLA_KEY), F32),
                   jax.ShapeDtypeStruct((n, D_MODEL), F32),
                   jax.ShapeDtypeStruct((n, GLA_KEY), F32)],
        compiler_params=_params("parallel"),
        name="proj_gla",
    )(x, w_g, w_lr, w_a2, b_a)


def _proj_gates_kernel(x_ref, w_ref, ga_ref, gb_ref):
    xb = x_ref[...].astype(BF16)
    z = jnp.dot(xb, w_ref[...], preferred_element_type=F32)
    dg, ma = z[:, :D_MODEL], z[:, D_MODEL:2 * D_MODEL]
    gg, mb = z[:, 2 * D_MODEL:3 * D_MODEL], z[:, 3 * D_MODEL:]
    ga_ref[...] = _sigmoid(ma) * (dg * _sigmoid(dg))
    gb_ref[...] = _sigmoid(mb) * (gg * _sigmoid(gg))


def _proj_gates(x, w_gates, tm):
    n = x.shape[0]
    row = lambda i: (i, 0)
    return pl.pallas_call(
        _proj_gates_kernel,
        grid=(n // tm,),
        in_specs=[pl.BlockSpec((tm, D_MODEL), row),
                  pl.BlockSpec((D_MODEL, 4 * D_MODEL), lambda i: (0, 0))],
        out_specs=[pl.BlockSpec((tm, D_MODEL), row)] * 2,
        out_shape=[jax.ShapeDtypeStruct((n, D_MODEL), F32)] * 2,
        compiler_params=_params("parallel"),
        name="proj_gates",
    )(x, w_gates)


def _lambda(lam_ref, lam_init):
    lp = lam_ref[...]
    a = jnp.sum(lp[0:1] * lp[1:2], axis=1, keepdims=True)
    b = jnp.sum(lp[2:3] * lp[3:4], axis=1, keepdims=True)
    return jnp.exp(a) - jnp.exp(b) + lam_init


def _head_norm(o, w, gain):
    return o * lax.rsqrt(jnp.mean(o * o, axis=-1, keepdims=True) + NORM_EPS) * w * gain


def _attn_prompt_kernel(qt_ref, k_ref, vt_ref, ga_ref, nw_ref, lam_ref, o_ref,
                        qq_scr, sa_scr, sb_scr, m_scr, l_scr, acc_scr, *, hg, tq, tk, lam_init):
    i = pl.program_id(2)
    row = lax.broadcasted_iota(jnp.int32, (LANES, tq), 0)
    for g in range(hg):
        qt = qt_ref[0, g * LANES:(g + 1) * LANES, :]
        zero = jnp.zeros_like(qt)
        qq_scr[g, :, :tq] = jnp.where(row < DA_DK, qt, zero)
        qq_scr[g, :, tq:] = jnp.where(row >= DA_DK, qt, zero)
    m_scr[...] = jnp.full(m_scr.shape, -jnp.inf, F32)
    l_scr[...] = jnp.zeros(l_scr.shape, F32)
    acc_scr[...] = jnp.zeros(acc_scr.shape, F32)

    def scores(j, s_ref):
        for g in range(hg):
            k = k_ref[0, pl.ds(pl.multiple_of(j * tk, tk), tk), g * LANES:(g + 1) * LANES]
            s_ref[g] = jnp.dot(k, qq_scr[g], preferred_element_type=F32)

    def softmax_pv(j, s_ref, masked):
        for g in range(hg):
            s = s_ref[g]
            if masked:
                r = lax.broadcasted_iota(jnp.int32, (tk, 2 * tq), 0)
                c = lax.broadcasted_iota(jnp.int32, (tk, 2 * tq), 1)
                qpos = i * tq + jnp.where(c >= tq, c - tq, c)
                s = jnp.where(j * tk + r <= qpos, s, MASK_VALUE)
            m_old = m_scr[g]
            m_new = jnp.maximum(m_old, jnp.max(s, axis=0, keepdims=True))
            alpha = jnp.exp2(m_old - m_new)
            p = jnp.exp2(s - m_new)
            l_scr[g] = alpha * l_scr[g] + jnp.sum(p, axis=0, keepdims=True)
            vt = vt_ref[0, j, g * LANES:(g + 1) * LANES, :]
            acc_scr[g] = alpha * acc_scr[g] + jnp.dot(vt, p.astype(BF16), preferred_element_type=F32)
            m_scr[g] = m_new

    assert tq == tk
    scores(0, sa_scr)

    def body(t, carry):
        scores(2 * t + 1, sb_scr)
        softmax_pv(2 * t, sa_scr, False)
        scores(2 * t + 2, sa_scr)
        softmax_pv(2 * t + 1, sb_scr, False)
        return carry
    lax.fori_loop(0, i // 2, body, 0)

    @pl.when(i % 2 == 1)
    def _():
        scores(i, sb_scr)
        softmax_pv(i - 1, sa_scr, False)
        softmax_pv(i, sb_scr, True)

    @pl.when(i % 2 == 0)
    def _():
        softmax_pv(i, sa_scr, True)

    lam = _lambda(lam_ref, lam_init)
    for g in range(hg):
        ot = acc_scr[g] / l_scr[g]
        o = (ot[:, :tq] - lam * ot[:, tq:]).T
        cols = slice(g * LANES, (g + 1) * LANES)
        o_ref[0, :, cols] = _head_norm(o, nw_ref[...], 1.0 - lam_init) * ga_ref[0, :, cols]


def _attn_prompt(qt, k, vt, ga, norm_w, lam_p, lam_init, tq, tk, hg):
    b, t, _ = k.shape
    nk = t // tk
    assert vt.shape == (b, nk, D_MODEL, tk) and DA_HEADS % hg == 0
    w = hg * LANES
    blk = pl.BlockSpec((1, tq, w), lambda bi, h, i: (bi, i, h))
    return pl.pallas_call(
        functools.partial(_attn_prompt_kernel, hg=hg, tq=tq, tk=tk, lam_init=lam_init),
        grid=(b, DA_HEADS // hg, t // tq),
        in_specs=[pl.BlockSpec((1, w, tq), lambda bi, h, i: (bi, h, i)),
                  pl.BlockSpec((1, t, w), lambda bi, h, i: (bi, 0, h)),
                  pl.BlockSpec((1, nk, w, tk), lambda bi, h, i: (bi, 0, h, 0)),
                  blk,
                  pl.BlockSpec((1, LANES), lambda bi, h, i: (0, 0)),
                  pl.BlockSpec((4, DA_DK), lambda bi, h, i: (0, 0))],
        out_specs=blk,
        out_shape=jax.ShapeDtypeStruct((b, t, D_MODEL), F32),
        scratch_shapes=[pltpu.VMEM((hg, LANES, 2 * tq), BF16),
                        pltpu.VMEM((hg, tk, 2 * tq), F32),
                        pltpu.VMEM((hg, tk, 2 * tq), F32),
                        pltpu.VMEM((hg, 1, 2 * tq), F32),
                        pltpu.VMEM((hg, 1, 2 * tq), F32),
                        pltpu.VMEM((hg, LANES, 2 * tq), F32)],
        compiler_params=_params("parallel", "parallel", "arbitrary"),
        name="attn_prompt",
    )(qt, k, vt, ga, norm_w, lam_p)


def _attn_sample_kernel(pt_ref, q_ref, kn_ref, vn_ref, ga_ref, nw_ref, lam_ref, *rest,
                        lam_init, t_new):
    k_pages = rest[:N_PAGES]
    v_pages = rest[N_PAGES:2 * N_PAGES]
    o_ref, kb_scr, vb_scr = rest[2 * N_PAGES:]
    n_rows = 2 * DA_HEADS * t_new
    past = N_PAGES * PAGE_SIZE

    for p in range(N_PAGES):
        kb_scr[:, p * PAGE_SIZE:(p + 1) * PAGE_SIZE] = k_pages[p][...].astype(BF16)
        for h in range(DA_HEADS):
            vh = v_pages[p][pl.ds(h, PAGE_SIZE, stride=DA_HEADS), :]
            vb_scr[p * PAGE_SIZE:(p + 1) * PAGE_SIZE, h * DA_DV:(h + 1) * DA_DV] = vh.astype(BF16)
    pad = jnp.zeros((PAGE_SIZE - t_new, D_MODEL), F32)
    k_new = jnp.concatenate([kn_ref[0], pad], axis=0).astype(BF16)
    vb_scr[past:, :] = jnp.concatenate([vn_ref[0], pad], axis=0).astype(BF16)

    qf = q_ref[0].astype(F32)
    q_rep = jnp.concatenate([qf] * (2 * DA_HEADS), axis=0)
    r = lax.broadcasted_iota(jnp.int32, (n_rows, D_MODEL), 0)
    c = lax.broadcasted_iota(jnp.int32, (n_rows, D_MODEL), 1)
    q_exp = jnp.where(c // DA_DK == r // t_new, q_rep, 0.0).astype(BF16)

    s_past = jnp.dot(q_exp, kb_scr[...], preferred_element_type=F32)
    s_new = lax.dot_general(q_exp, k_new, NT_DIMS, preferred_element_type=F32)
    rr = lax.broadcasted_iota(jnp.int32, s_new.shape, 0)
    cc = lax.broadcasted_iota(jnp.int32, s_new.shape, 1)
    s_new = jnp.where(cc <= rr % t_new, s_new, MASK_VALUE)
    m = jnp.maximum(jnp.max(s_past, axis=1, keepdims=True), jnp.max(s_new, axis=1, keepdims=True))
    p_past = jnp.exp(s_past - m)
    p_new = jnp.exp(s_new - m)
    l = jnp.sum(p_past, axis=1, keepdims=True) + jnp.sum(p_new, axis=1, keepdims=True)
    p = jnp.concatenate([p_past, p_new], axis=1).astype(BF16)
    out = jnp.dot(p, vb_scr[...], preferred_element_type=F32) / l

    lam = _lambda(lam_ref, lam_init)
    heads = []
    for h in range(DA_HEADS):
        cols = slice(h * DA_DV, (h + 1) * DA_DV)
        o0 = out[(2 * h) * t_new:(2 * h + 1) * t_new, cols]
        o1 = out[(2 * h + 1) * t_new:(2 * h + 2) * t_new, cols]
        heads.append(_head_norm(o0 - lam * o1, nw_ref[...], 1.0 - lam_init))
    o_ref[0] = jnp.concatenate(heads, axis=1) * ga_ref[0]


def _attn_sample(page_table, q, k_new, v_new, ga, norm_w, lam_p, cache_kt, cache_v, layer, lam_init):
    nb, t_new, _ = q.shape
    tok = pl.BlockSpec((1, t_new, D_MODEL), lambda bi, pt: (bi, 0, 0))

    def page_spec(p):
        return pl.BlockSpec((None, None, D_MODEL, PAGE_SIZE),
                            lambda bi, pt: (layer, pt[bi * N_PAGES + p], 0, 0))

    pages = [page_spec(p) for p in range(N_PAGES)]
    return pl.pallas_call(
        functools.partial(_attn_sample_kernel, lam_init=lam_init, t_new=t_new),
        grid_spec=pltpu.PrefetchScalarGridSpec(
            num_scalar_prefetch=1,
            grid=(nb,),
            in_specs=[tok, tok, tok, tok,
                      pl.BlockSpec((1, LANES), lambda bi, pt: (0, 0)),
                      pl.BlockSpec((4, DA_DK), lambda bi, pt: (0, 0))] + pages + pages,
            out_specs=tok,
            scratch_shapes=[pltpu.VMEM((D_MODEL, N_PAGES * PAGE_SIZE), BF16),
                            pltpu.VMEM(((N_PAGES + 1) * PAGE_SIZE, D_MODEL), BF16)]),
        out_shape=jax.ShapeDtypeStruct((nb, t_new, D_MODEL), F32),
        compiler_params=_params("arbitrary"),
        name="attn_sample",
    )(page_table.reshape(-1), q, k_new, v_new, ga, norm_w, lam_p,
      *([cache_kt] * N_PAGES), *([cache_v] * N_PAGES))


def _gla_kernel(*refs, bb, t_blk, rows, chunk, has_s0):
    q_ref, k_ref, v_ref, la_ref, gb_ref, nw_ref = refs[:6]
    y_ref, sf_ref, s_scr = refs[-3:]
    blk = pl.program_id(1)
    n_chunks = rows // chunk
    chains = [(bi, h) for bi in range(bb) for h in range(GLA_HEADS)]

    @pl.when(blk == 0)
    def _():
        if has_s0:
            s_scr[...] = refs[6][...]
        else:
            s_scr[...] = jnp.zeros(s_scr.shape, F32)

    def load(ref, bi, lo, width):
        x = ref[bi, :, lo:lo + width]
        if t_blk < rows:
            x = jnp.concatenate([x, jnp.zeros((rows - t_blk, width), F32)], axis=0)
        return x

    r = lax.broadcasted_iota(jnp.int32, (rows, rows), 0)
    c = lax.broadcasted_iota(jnp.int32, (rows, rows), 1)
    tri = (r // chunk == c // chunk) & (c <= r)
    pos_in_chunk = lax.broadcasted_iota(jnp.int32, (rows, GLA_DK), 0) % chunk
    assert chunk == 32

    q_dec, k_end, vb, o_intra, decay_t, s = [], [], [], [], [], []
    for bi, h in chains:
        q, k, la = (load(ref, bi, h * GLA_DK, GLA_DK) for ref in (q_ref, k_ref, la_ref))
        v = load(v_ref, bi, h * GLA_DV, GLA_DV)
        b = la
        for sh in (1, 2, 4, 8, 16):
            b = b + jnp.where(pos_in_chunk >= sh, pltpu.roll(b, sh, 0), 0.0)
        qd = (q * jnp.exp(b) * (GLA_DK ** -0.5)).astype(BF16)
        k_dec = (k * jnp.exp(-b)).astype(BF16)
        a = lax.dot_general(qd, k_dec, NT_DIMS, preferred_element_type=F32)
        a = jnp.where(tri, a, 0.0)
        vb.append(v.astype(BF16))
        q_dec.append(qd)
        o_intra.append(jnp.dot(a.astype(BF16), vb[-1], preferred_element_type=F32))
        b3 = b.reshape(n_chunks, chunk, GLA_DK)
        b_last = b3[:, chunk - 1:chunk, :]
        k_end.append((k.reshape(n_chunks, chunk, GLA_DK) * jnp.exp(b_last - b3))
                     .reshape(rows, GLA_DK).astype(BF16))
        decay = jnp.exp(b_last.reshape(n_chunks, GLA_DK))
        if n_chunks == 1:
            decay = jnp.broadcast_to(decay, (SUBLANES, GLA_DK))
        decay_t.append(decay.T)
        s.append(s_scr[bi, h])

    outs = [[] for _ in chains]
    for ci in range(n_chunks):
        sl = slice(ci * chunk, (ci + 1) * chunk)
        for n in range(len(chains)):
            outs[n].append(jnp.dot(q_dec[n][sl], s[n].astype(BF16), preferred_element_type=F32))
            ds = lax.dot_general(k_end[n][sl], vb[n][sl], TN_DIMS, preferred_element_type=F32)
            s[n] = decay_t[n][:, ci:ci + 1] * s[n] + ds

    for n, (bi, h) in enumerate(chains):
        s_scr[bi, h] = s[n]
        o = o_intra[n] + (jnp.concatenate(outs[n], axis=0) if n_chunks > 1 else outs[n][0])
        y = o * lax.rsqrt(jnp.mean(o * o, axis=-1, keepdims=True) + NORM_EPS) * nw_ref[...]
        cols = slice(h * GLA_DV, (h + 1) * GLA_DV)
        y_ref[bi, :, cols] = y[:t_blk] * gb_ref[bi, :, cols]

    @pl.when(blk == pl.num_programs(1) - 1)
    def _():
        for n, (bi, h) in enumerate(chains):
            sf_ref[bi, h] = s[n]


def _gla(gq, gk, gv, la, gb, norm_w, state, layer, prev_states, t_blk, bb):
    b, t, _ = gq.shape
    rows = max(t_blk, GLA_CHUNK)
    assert rows // GLA_CHUNK in (1, SUBLANES) and b % bb == 0
    kspec = pl.BlockSpec((bb, t_blk, GLA_KEY), lambda bi, i: (bi, i, 0))
    vspec = pl.BlockSpec((bb, t_blk, D_MODEL), lambda bi, i: (bi, i, 0))
    sspec = pl.BlockSpec((None, bb, GLA_HEADS, GLA_DK, GLA_DV), lambda bi, i: (layer, bi, 0, 0, 0))
    in_specs = [kspec, kspec, vspec, kspec, vspec, pl.BlockSpec((1, GLA_DV), lambda bi, i: (0, 0))]
    args = [gq, gk, gv, la, gb, norm_w]
    if state is not None:
        in_specs.append(sspec)
        args.append(state)
    aliases = {}
    if prev_states is not None:
        aliases = {len(args): 1}
        in_specs.append(pl.BlockSpec(memory_space=pl.ANY))
        args.append(prev_states)
    return pl.pallas_call(
        functools.partial(_gla_kernel, bb=bb, t_blk=t_blk, rows=rows, chunk=GLA_CHUNK,
                          has_s0=state is not None),
        grid=(b // bb, t // t_blk),
        in_specs=in_specs,
        out_specs=[vspec, sspec],
        out_shape=[jax.ShapeDtypeStruct((b, t, D_MODEL), F32),
                   jax.ShapeDtypeStruct((DEPTH, b, GLA_HEADS, GLA_DK, GLA_DV), F32)],
        scratch_shapes=[pltpu.VMEM((bb, GLA_HEADS, GLA_DK, GLA_DV), F32)],
        input_output_aliases=aliases,
        compiler_params=_params("parallel", "arbitrary"),
        name="gla",
    )(*args)


def _out_proj_kernel(ya_ref, yb_ref, x_ref, w_ref, lw_ref, lb_ref, o_ref):
    merged = (ya_ref[...] + yb_ref[...]).astype(BF16)
    h = jnp.dot(merged, w_ref[...], preferred_element_type=F32)
    y = DEEPNORM_ALPHA * x_ref[...] + h
    mu = jnp.mean(y, axis=-1, keepdims=True)
    d = y - mu
    var = jnp.mean(d * d, axis=-1, keepdims=True)
    o_ref[...] = d * lax.rsqrt(var + NORM_EPS) * lw_ref[...] + lb_ref[...]


def _out_proj(ya, yb, x, w_out, ln_w, ln_b, tm):
    n = x.shape[0]
    row = lambda i: (i, 0)
    const = lambda i: (0, 0)
    return pl.pallas_call(
        _out_proj_kernel,
        grid=(n // tm,),
        in_specs=[pl.BlockSpec((tm, D_MODEL), row)] * 3
                 + [pl.BlockSpec((D_MODEL, D_MODEL), const),
                    pl.BlockSpec((1, D_MODEL), const), pl.BlockSpec((1, D_MODEL), const)],
        out_specs=pl.BlockSpec((tm, D_MODEL), row),
        out_shape=jax.ShapeDtypeStruct((n, D_MODEL), F32),
        compiler_params=_params("parallel"),
        name="out_proj",
    )(ya, yb, x, w_out, ln_w, ln_b)


def _rope_tables(pos):
    half = DA_DK // 2
    inv = ROPE_THETA ** (-jnp.arange(0, DA_DK, 2, dtype=F32) / DA_DK)
    ang = pos.astype(F32)[:, None] * inv[None, :]
    cos32, sin32 = jnp.cos(ang), jnp.sin(ang)
    cos = jnp.tile(cos32, (1, LANES // half))
    sin = jnp.tile(sin32, (1, LANES // half))
    first = (jnp.arange(LANES) % DA_DK) < half
    return (cos, jnp.where(first, -sin, 0.0), jnp.where(first, 0.0, sin)), (cos32.T, sin32.T)


def _layer_weights(layer, w_in, gla_w_a2, gla_b_a, w_out):
    w = w_in[layer]
    o = [0]
    for s in (1024, 1024, 1024, 1024, 512, 512, 1024, 1024, 16, 1024, 1024):
        o.append(o[-1] + s)
    seg = lambda i: w[:, o[i]:o[i + 1]]
    return dict(
        wqt=seg(0).T.astype(BF16), wkt=seg(1).T.astype(BF16), wvt=seg(2).T.astype(BF16),
        w_g=jnp.concatenate([seg(4), seg(5), seg(6)], axis=1).astype(BF16),
        w_gates=jnp.concatenate([seg(3), seg(9), seg(7), seg(10)], axis=1).astype(BF16),
        w_lr=jnp.pad(seg(8), ((0, 0), (0, LANES - GLA_RANK))).astype(BF16),
        w_a2=jnp.pad(gla_w_a2[layer], ((0, LANES - GLA_RANK), (0, 0))).astype(BF16),
        b_a=gla_b_a[layer][None, :],
        w_out=w_out[layer].astype(BF16))


def kernel(x_prompt, x_sample, cache_k, cache_v, state_gla, page_table, w_in,
           lam_q1, lam_k1, lam_q2, lam_k2, da_norm_w, gla_w_a2, gla_b_a, gla_norm_w,
           w_out, ln_w, ln_b):
    bp, tp, _ = x_prompt.shape
    bs, ts, _ = x_sample.shape
    ns = bs * ts
    tm = 512
    tm_s = min(tm, ns)
    _, tabt_p = _rope_tables(jnp.arange(tp))
    tab_s, _ = _rope_tables(PAST_LEN + jnp.arange(ts))
    tab_s = tuple(jnp.tile(t, (tm_s // ts, 1)) for t in tab_s)
    xp = x_prompt.reshape(bp * tp, D_MODEL)
    xs = x_sample.reshape(ns, D_MODEL)
    n_pool = cache_k.shape[1]
    cache_kt = jnp.transpose(cache_k, (0, 1, 3, 4, 5, 2)).reshape(DEPTH, n_pool, D_MODEL, PAGE_SIZE)
    cache_v2 = cache_v.reshape(DEPTH, n_pool, PAGE_SIZE * DA_HEADS, DA_DV)

    ks_rows, vs_rows = [], []
    kv_p = sp_all = ss_all = None
    for layer in range(DEPTH):
        lam_init = 0.8 - 0.6 * math.exp(-0.3 * layer)
        w = _layer_weights(layer, w_in, gla_w_a2, gla_b_a, w_out)
        lam_p = jnp.stack([lam_q1[layer], lam_k1[layer], lam_q2[layer], lam_k2[layer]])
        da_w = da_norm_w[layer][None, :]
        gla_w = gla_norm_w[layer][None, :]
        lw, lb = ln_w[layer][None, :], ln_b[layer][None, :]
        sh3 = lambda a, b: a.reshape(b, -1, a.shape[-1])

        qt, kt_all, kb, vf_all, vtb = _proj_attn(sh3(xp, bp), w["wqt"], w["wkt"], w["wvt"], tabt_p, tm,
                                                 True, layer, kv_p)
        kv_p = (kt_all, vf_all)
        gq_p, gk_p, gv_p, la_p = _proj_gla(xp, w["w_g"], w["w_lr"], w["w_a2"], w["b_a"], tm)
        ga_p, gb_p = _proj_gates(xp, w["w_gates"], tm)
        q, kf, vf = _proj_attn(sh3(xs, 1), w["wqt"], w["wkt"], w["wvt"], tab_s, tm_s, False)
        gq_s, gk_s, gv_s, la_s = _proj_gla(xs, w["w_g"], w["w_lr"], w["w_a2"], w["b_a"], tm_s)
        ga_s, gb_s = _proj_gates(xs, w["w_gates"], tm_s)
        ya_p = _attn_prompt(qt, kb, vtb, sh3(ga_p, bp), da_w, lam_p, lam_init, tq=tm, tk=tm, hg=1)
        ya_s = _attn_sample(page_table, sh3(q, bs), sh3(kf, bs), sh3(vf, bs), sh3(ga_s, bs), da_w, lam_p,
                            cache_kt, cache_v2, layer, lam_init)

        yb, sp_all = _gla(sh3(gq_p, bp), sh3(gk_p, bp), sh3(gv_p, bp), sh3(la_p, bp), sh3(gb_p, bp), gla_w,
                          None, layer, sp_all, 256, bp)
        xp = _out_proj(ya_p.reshape(-1, D_MODEL), yb.reshape(-1, D_MODEL), xp, w["w_out"], lw, lb, tm)
        yb, ss_all = _gla(sh3(gq_s, bs), sh3(gk_s, bs), sh3(gv_s, bs), sh3(la_s, bs), sh3(gb_s, bs), gla_w,
                          state_gla, layer, ss_all, ts, 8)
        xs = _out_proj(ya_s.reshape(-1, D_MODEL), yb.reshape(-1, D_MODEL), xs, w["w_out"], lw, lb, tm_s)
        ks_rows.append(kf.reshape(bs, ts, DA_HEADS, 2, DA_DK))
        vs_rows.append(vf.reshape(bs, ts, DA_HEADS, DA_DV))

    kt_all, vf_all = kv_p
    new_k_p = jnp.transpose(kt_all.reshape(DEPTH, bp, DA_HEADS, 2, DA_DK, tp), (0, 1, 5, 2, 3, 4))
    return (xp.reshape(bp, tp, D_MODEL), xs.reshape(bs, ts, D_MODEL),
            new_k_p, vf_all.reshape(DEPTH, bp, tp, DA_HEADS, DA_DV), sp_all,
            jnp.stack(ks_rows), jnp.stack(vs_rows), ss_all)
```
